```python
import jax, jax.numpy as jnp
from jax import lax
import numpy as np

D_MODEL = 1024
BATCH = 8
SEQ = 4096
DEPTH = 4

CTX_LEN = 256
GRID_W = 64
D_MIX = D_MODEL
HG_WIDTH = D_MIX // 2
HG_HEADS = 4
HG_DK = HG_WIDTH // HG_HEADS
HG_DV = HG_DK
HG_CHUNK = 64
MLP_WIDTH = D_MIX // 4
MLP_GROUPS = 4
MLP_CH = MLP_WIDTH // MLP_GROUPS
TOK_CHUNK = 128
CONV_WIDTH = D_MIX // 4
CONV_K = 3
D_FF = 4 * D_MODEL
N_MOD = 6
EPS = 1e-6
OFF_Q = 0
OFF_I = OFF_Q + HG_WIDTH
OFF_FF = OFF_I + HG_WIDTH
OFF_FB = OFF_FF + HG_WIDTH
OFF_G = OFF_FB + HG_WIDTH
OFF_U = OFF_G + HG_WIDTH
OFF_V = OFF_U + MLP_WIDTH
OFF_CH = OFF_V + MLP_WIDTH
OFF_CB = OFF_CH + CONV_WIDTH
OFF_CC = OFF_CB + CONV_WIDTH
IN_COLS = OFF_CC + CONV_WIDTH
HG_SCAN_COLS = OFF_G

kernel_name = 'hybrid_hgrn2_gmlp_shortconv_dit'


def rmsnorm(x, gain):
    xf = x.astype(jnp.float32)
    y = xf * lax.rsqrt(jnp.mean(xf * xf, axis=-1, keepdims=True) + EPS)
    return (y * gain.astype(jnp.float32)).astype(x.dtype)


def ada_mod(cond, w, b):
    m = jax.nn.silu(cond) @ w + b
    m = m.reshape(cond.shape[:-1] + (1, N_MOD, D_MODEL))
    return (m[..., 0, :], m[..., 1, :], m[..., 2, :], m[..., 3, :], m[..., 4, :], m[..., 5, :])


def modulate(h, shift, scale):
    return h * (1.0 + scale) + shift


def lower_bounds(lb_raw):
    p = jax.nn.softmax(lb_raw.astype(jnp.float32), axis=1)
    return jnp.cumsum(p, axis=1) - p[:, :1]


def forget_gate(zf, lb):
    lb = lb.reshape(HG_HEADS, HG_DK)
    f = lb + (1.0 - lb) * jax.nn.sigmoid(zf)
    logf = jnp.log(f)
    k = (1.0 - lb) * jax.nn.sigmoid(-zf)
    return logf, k


def hgrn2_scan(q, k, v, logf, s0):
    Bsz, N = q.shape[:2]
    nc = N // HG_CHUNK

    def chunks(a):
        return jnp.moveaxis(a.reshape((Bsz, nc, HG_CHUNK) + a.shape[2:]), 1, 0)

    incl = jnp.tril(jnp.ones((HG_CHUNK, HG_CHUNK), dtype=bool))[None, :, :, None, None]

    def step(S, inp):
        qc, kc, vc, lf = inp
        A = jnp.cumsum(lf, axis=1)
        A_end = A[:, -1]
        o_inter = jnp.einsum('blhk,bhkv->blhv', qc * jnp.exp(A), S)
        diff = jnp.where(incl, A[:, :, None] - A[:, None, :], 0.0)
        decay = jnp.where(incl, jnp.exp(diff), 0.0)
        scores = jnp.einsum('btshk,bshk->btsh', decay * qc[:, :, None], kc)
        o_intra = jnp.einsum('btsh,bshv->bthv', scores, vc)
        S_new = jnp.exp(A_end)[..., None] * S + jnp.einsum('bshk,bshv->bhkv', kc * jnp.exp(A_end[:, None] - A), vc)
        return S_new, o_inter + o_intra

    S_fin, o = lax.scan(step, s0, (chunks(q), chunks(k), chunks(v), chunks(logf)))
    return jnp.moveaxis(o, 0, 1).reshape(Bsz, N, HG_HEADS, HG_DV), S_fin


def hgrn2_bidir(z, lb_f, lb_b, s0_f, s0_b):
    Bsz, N = z.shape[:2]

    def heads(a):
        return a.astype(jnp.float32).reshape(Bsz, N, HG_HEADS, HG_DK)

    q = heads(z[..., OFF_Q:OFF_I])
    v = heads(z[..., OFF_I:OFF_FF])
    logf_f, k_f = forget_gate(heads(z[..., OFF_FF:OFF_FB]), lb_f)
    logf_b, k_b = forget_gate(heads(z[..., OFF_FB:OFF_G]), lb_b)
    o_f, s_f = hgrn2_scan(q, k_f, v, logf_f, s0_f)
    o_b, s_b = hgrn2_scan(jnp.flip(q, 1), jnp.flip(k_b, 1), jnp.flip(v, 1), jnp.flip(logf_b, 1), s0_b)
    return o_f + jnp.flip(o_b, 1), s_f, s_b


def chunk_token_mlp(u, v, ws, bs):
    Bsz, N, _ = u.shape
    nch = N // TOK_CHUNK
    shp = (Bsz, nch, TOK_CHUNK, MLP_GROUPS, MLP_CH)
    uf = jax.nn.gelu(u.astype(jnp.float32)).reshape(shp)
    vf = jax.nn.gelu(v.astype(jnp.float32)).reshape(shp)
    vf = vf * lax.rsqrt(jnp.mean(vf * vf, axis=-1, keepdims=True) + EPS)
    s = jnp.einsum('gts,bnsgc->bntgc', ws.astype(jnp.float32), vf) + bs.astype(jnp.float32).T[None, None, :, :, None]
    return (uf * s).reshape(Bsz, N, MLP_WIDTH).astype(u.dtype)


def short_conv(h, gB, gC, conv_w, col):
    y = gC * h
    y_prev = jnp.pad(y, ((0, 0), (1, 0), (0, 0)))[:, :-1]
    y_next = jnp.pad(y, ((0, 0), (0, 1), (0, 0)))[:, 1:]
    if col is not None:
        y_prev = jnp.where((col != 0)[None, :, None], y_prev, 0.0)
        y_next = jnp.where((col != GRID_W - 1)[None, :, None], y_next, 0.0)
    return gB * (conv_w[0] * y_prev + conv_w[1] * y + conv_w[2] * y_next)


def mixer(h, w_in, lb_f, lb_b, hg_gain, ws, bs, conv_w, w_out, s0_f, s0_b, col):
    Bsz, N, _ = h.shape
    z = h @ w_in
    o, s_f, s_b = hgrn2_bidir(z[..., :HG_SCAN_COLS], lb_f, lb_b, s0_f, s0_b)
    o = o * lax.rsqrt(jnp.mean(o * o, axis=-1, keepdims=True) + EPS)
    o = o.reshape(Bsz, N, HG_WIDTH) * hg_gain.astype(jnp.float32) * jax.nn.silu(z[..., OFF_G:OFF_U].astype(jnp.float32))
    m = chunk_token_mlp(z[..., OFF_U:OFF_V], z[..., OFF_V:OFF_CH], ws, bs)
    cv = short_conv(z[..., OFF_CH:OFF_CB], z[..., OFF_CB:OFF_CC], z[..., OFF_CC:IN_COLS], conv_w, col)
    out = jnp.concatenate([o.astype(h.dtype), m, cv], axis=-1) @ w_out
    return out, s_f, s_b


def sq_relu_mlp(h, w1, w2):
    a = jax.nn.relu(h @ w1)
    return (a * a) @ w2


def setup_inputs(seed: int = 0) -> dict:
    key = jax.random.key(seed)
    ks = jax.random.split(key, 20)
    f32 = jnp.float32
    nrm = lambda k, shp, s: jax.random.normal(k, shp, f32) * s
    return {
        'x': nrm(ks[0], (BATCH, SEQ, D_MODEL), 1.0),
        'c': nrm(ks[1], (BATCH, D_MODEL), 1.0),
        'ctx': nrm(ks[2], (BATCH, CTX_LEN, D_MODEL), 1.0),
        'c_ctx': nrm(ks[3], (D_MODEL,), 1.0),
        'norm1': 1.0 + nrm(ks[4], (DEPTH, D_MODEL), 0.02),
        'norm2': 1.0 + nrm(ks[5], (DEPTH, D_MODEL), 0.02),
        'ada_w': nrm(ks[6], (DEPTH, D_MODEL, N_MOD * D_MODEL), 0.5 * D_MODEL ** -0.5),
        'ada_b': nrm(ks[7], (DEPTH, N_MOD * D_MODEL), 0.02),
        'w_in': nrm(ks[8], (DEPTH, D_MODEL, IN_COLS), D_MODEL ** -0.5),
        'lb_raw': nrm(ks[9], (2, DEPTH, HG_WIDTH), 0.1),
        'hg_gain': 1.0 + nrm(ks[10], (DEPTH, HG_WIDTH), 0.02),
        'tok_ws': nrm(ks[11], (DEPTH, MLP_GROUPS, TOK_CHUNK, TOK_CHUNK), TOK_CHUNK ** -0.5),
        'tok_bs': nrm(ks[12], (DEPTH, MLP_GROUPS, TOK_CHUNK), 0.02),
        'conv_w': nrm(ks[13], (DEPTH, CONV_K, CONV_WIDTH), CONV_K ** -0.5),
        'w_out': nrm(ks[14], (DEPTH, D_MIX, D_MODEL), D_MIX ** -0.5),
        'w1': nrm(ks[15], (DEPTH, D_MODEL, D_FF), D_MODEL ** -0.5),
        'w2': nrm(ks[16], (DEPTH, D_FF, D_MODEL), D_FF ** -0.5),
        'final_norm': 1.0 + nrm(ks[17], (D_MODEL,), 0.02),
    }


def reference(x, c, ctx, c_ctx, norm1, norm2, ada_w, ada_b, w_in, lb_raw, hg_gain, tok_ws, tok_bs, conv_w, w_out, w1, w2, final_norm):
    Bsz = x.shape[0]
    rows = x.shape[1] // GRID_W
    col = jnp.arange(rows * GRID_W) % GRID_W
    lbs = lower_bounds(lb_raw)
    s_zero = jnp.zeros((Bsz, HG_HEADS, HG_DK, HG_DV), jnp.float32)
    xl, xc = x, ctx
    for l in range(DEPTH):
        shl1, scl1, gl1, shl2, scl2, gl2 = ada_mod(c, ada_w[l], ada_b[l])
        shc1, scc1, gc1, shc2, scc2, gc2 = ada_mod(c_ctx, ada_w[l], ada_b[l])
        mix_w = (w_in[l], lbs[0, l], lbs[1, l], hg_gain[l], tok_ws[l], tok_bs[l], conv_w[l], w_out[l])
        hc = modulate(rmsnorm(xc, norm1[l]), shc1, scc1)
        if l < DEPTH - 1:
            out_c, s_f, s_b = mixer(hc, *mix_w, s_zero, s_zero, None)
            xc = xc + gc1 * out_c
            xc = xc + gc2 * sq_relu_mlp(modulate(rmsnorm(xc, norm2[l]), shc2, scc2), w1[l], w2[l])
        else:
            _, s_f, s_b = hgrn2_bidir(hc @ w_in[l][:, :HG_SCAN_COLS], lbs[0, l], lbs[1, l], s_zero, s_zero)
        hl = modulate(rmsnorm(xl, norm1[l]), shl1, scl1)
        out_l, _, _ = mixer(hl, *mix_w, s_f, s_b, col)
        xl = xl + gl1 * out_l
        xl = xl + gl2 * sq_relu_mlp(modulate(rmsnorm(xl, norm2[l]), shl2, scl2), w1[l], w2[l])
    return rmsnorm(xl, final_norm)
```

```python
import functools
import math

import jax
import jax.numpy as jnp
from jax import lax
from jax.experimental import pallas as pl
from jax.experimental.pallas import tpu as pltpu

D_MODEL = 1024
DEPTH = 4
CTX_LEN = 256
GRID_W = 64
HG_WIDTH = 512
HG_HEADS = 4
HG_DK = 128
MLP_WIDTH = 256
MLP_GROUPS = 4
MLP_CH = 64
TOK_CHUNK = 128
CONV_WIDTH = 256
D_FF = 4096
N_MOD = 6
EPS = 1e-6
IN_COLS = 3840

TILE = 256
CHUNK = 64
N_CHUNK = TILE // CHUNK
MID = CHUNK // 2
SAFE_EXP = 64.0
COND_ROWS = 16
ADA_TN = 1536
FF_CHUNK = 1024
VMEM_LIMIT = 56 * 1024 * 1024

F32 = jnp.float32
BF16 = jnp.bfloat16


def _dot(a, b):
    return jnp.dot(a, b, preferred_element_type=F32)


def _dot_nt(a, b):
    return lax.dot_general(a, b, (((1,), (1,)), ((), ())), preferred_element_type=F32)


def _dot_tn(a, b):
    return lax.dot_general(a, b, (((0,), (0,)), ((), ())), preferred_element_type=F32)


def _rms(x, gain):
    return x * lax.rsqrt(jnp.mean(x * x, axis=-1, keepdims=True) + EPS) * gain


def _gelu(x):
    c = math.sqrt(2.0 / math.pi)
    return x * (0.5 * (1.0 + jnp.tanh(c * (x + 0.044715 * (x * x * x)))))


def _silu(x):
    return x / (1.0 + jnp.exp(-x))


def _ada_kernel(cond_ref, w_ref, b_ref, out_ref):
    s = _silu(cond_ref[...]).astype(BF16)
    out_ref[...] = _dot(s, w_ref[...].astype(BF16)) + b_ref[...]


def _ada_call(cond, ada_w, ada_b):
    n_cols = N_MOD * D_MODEL
    return pl.pallas_call(
        _ada_kernel,
        grid=(DEPTH, n_cols // ADA_TN),
        in_specs=[
            pl.BlockSpec((COND_ROWS, D_MODEL), lambda l, n: (0, 0)),
            pl.BlockSpec((None, D_MODEL, ADA_TN), lambda l, n: (l, 0, n)),
            pl.BlockSpec((None, 1, ADA_TN), lambda l, n: (l, 0, n)),
        ],
        out_specs=pl.BlockSpec((None, COND_ROWS, ADA_TN), lambda l, n: (l, 0, n)),
        out_shape=jax.ShapeDtypeStruct((DEPTH, COND_ROWS, n_cols), F32),
        compiler_params=pltpu.CompilerParams(
            dimension_semantics=("arbitrary", "arbitrary"), vmem_limit_bytes=VMEM_LIMIT),
        name="ada_mod",
    )(cond, ada_w, ada_b.reshape(DEPTH, 1, n_cols))


def _inproj_kernel(x_ref, mod_ref, gain_ref, w_ref, z_ref):
    shift = mod_ref[:, 0:D_MODEL]
    scale = mod_ref[:, D_MODEL:2 * D_MODEL]
    h = _rms(x_ref[...], gain_ref[...]) * (1.0 + scale) + shift
    z_ref[...] = _dot(h.astype(BF16), w_ref[...])


def _inproj_call(xall, mods_l, gain, w_in_bf16, t0):
    n_batch, n_tok, _ = xall.shape
    n_tiles = n_tok // TILE - t0
    return pl.pallas_call(
        _inproj_kernel,
        grid=(n_batch, n_tiles),
        in_specs=[
            pl.BlockSpec((None, TILE, D_MODEL), lambda b, t: (b, t + t0, 0)),
            pl.BlockSpec((None, 1, N_MOD * D_MODEL),
                         lambda b, t: (jnp.where(t + t0 == 0, n_batch, b), 0, 0)),
            pl.BlockSpec((1, D_MODEL), lambda b, t: (0, 0)),
            pl.BlockSpec((D_MODEL, IN_COLS), lambda b, t: (0, 0), pipeline_mode=pl.Buffered(1)),
        ],
        out_specs=pl.BlockSpec((None, TILE, IN_COLS), lambda b, t: (b, t + t0, 0)),
        out_shape=jax.ShapeDtypeStruct((n_batch, n_tok, IN_COLS), F32),
        compiler_params=pltpu.CompilerParams(
            dimension_semantics=("arbitrary", "arbitrary"), vmem_limit_bytes=VMEM_LIMIT),
        name="in_proj",
    )(xall, mods_l, gain, w_in_bf16)


def _gates(zf, lb):
    e = jnp.exp(-jnp.abs(zf))
    r = 1.0 / (1.0 + e)
    er = e * r
    pos = zf >= 0.0
    sig = jnp.where(pos, r, er)
    nsig = jnp.where(pos, er, r)
    oml = 1.0 - lb
    return jnp.log(lb + oml * sig), oml * nsig


def _cumsum_rows(tri_bf16, lf):
    hi = lf.astype(BF16)
    r1 = lf - hi.astype(F32)
    mid = r1.astype(BF16)
    lo = (r1 - mid.astype(F32)).astype(BF16)
    return _dot(tri_bf16, hi) + _dot(tri_bf16, mid) + _dot(tri_bf16, lo)


def _tri(fwd):
    t = lax.broadcasted_iota(jnp.int32, (CHUNK, CHUNK), 0)
    s = lax.broadcasted_iota(jnp.int32, (CHUNK, CHUNK), 1)
    return (s <= t) if fwd else (s >= t)


def _scan_kernel(layer, zf_ref, zbqi_ref, zbfb_ref, lbraw_ref, of_ref, ob_ref,
                 st_ref, lb_ref, p_ref, a_ref, k_ref):
    j = pl.program_id(1)

    @pl.when(j == 0)
    def _():
        st_ref[...] = jnp.zeros_like(st_ref)
        for d in range(2):
            rows = [lbraw_ref[d * DEPTH + i:d * DEPTH + i + 1, :] for i in range(DEPTH)]
            top = functools.reduce(jnp.maximum, rows)
            ex = [jnp.exp(row - top) for row in rows]
            part = functools.reduce(lambda u, w: u + w, ex[1:layer + 1], jnp.zeros_like(top))
            lb_ref[d:d + 1, :] = part / functools.reduce(lambda u, w: u + w, ex)

    def operands(fwd, c):
        rows = slice(c * CHUNK, (c + 1) * CHUNK)
        if fwd:
            q = zf_ref[rows, 0:HG_WIDTH]
            v = zf_ref[rows, HG_WIDTH:2 * HG_WIDTH]
            z = zf_ref[rows, 2 * HG_WIDTH:3 * HG_WIDTH]
            lb = lb_ref[0:1, :]
        else:
            q = zbqi_ref[rows, 0:HG_WIDTH]
            v = zbqi_ref[rows, HG_WIDTH:2 * HG_WIDTH]
            z = zbfb_ref[rows, :]
            lb = lb_ref[1:2, :]
        return q, v, z, lb

    def decay_terms(fwd, z, lb):
        lf, k = _gates(z, lb)
        a = _cumsum_rows(_tri(fwd).astype(BF16), lf)
        tot = a[CHUNK - 1:CHUNK, :] if fwd else a[0:1, :]
        return a, k, tot

    worst = jnp.zeros((1, HG_WIDTH), F32)
    for step in range(N_CHUNK):
        for fwd in (True, False):
            c = step if fwd else N_CHUNK - 1 - step
            d = 0 if fwd else 1
            o_ref = of_ref if fwd else ob_ref
            q, v, z, lb = operands(fwd, c)
            a, k, tot = decay_terms(fwd, z, lb)
            ref_row = a[MID:MID + 1, :]
            worst = jnp.maximum(worst, jnp.maximum(jnp.abs(a[0:1, :] - ref_row),
                                                   jnp.abs(a[CHUNK - 1:CHUNK, :] - ref_row)))
            q_in = (q * jnp.exp(a)).astype(BF16)
            q_mid = (q * jnp.exp(a - ref_row)).astype(BF16)
            k_mid = (k * jnp.exp(ref_row - a)).astype(BF16)
            k_end = (k * jnp.exp(tot - a)).astype(BF16)
            d_tot = jnp.exp(tot)
            v_bf = v.astype(BF16)
            mask = _tri(fwd)
            for h in range(HG_HEADS):
                ln = slice(h * HG_DK, (h + 1) * HG_DK)
                idx = d * HG_HEADS + h
                st = st_ref[idx]
                o_ref[c * CHUNK:(c + 1) * CHUNK, ln] = _dot_nt(q_in[:, ln], st.astype(BF16))
                p = _dot_nt(q_mid[:, ln], k_mid[:, ln])
                p_ref[(step * 2 + d) * HG_HEADS + h] = jnp.where(mask, p, 0.0)
                st_ref[idx] = st * d_tot[:, ln] + _dot_tn(v_bf[:, ln], k_end[:, ln])

    @pl.when(jnp.max(worst) > SAFE_EXP)
    def _():
        t_col = lax.broadcasted_iota(jnp.int32, (CHUNK, 1), 0)
        s_row = lax.broadcasted_iota(jnp.int32, (1, CHUNK), 1)
        sub8 = lax.broadcasted_iota(jnp.int32, (8, 1), 0)
        for step in range(N_CHUNK):
            for fwd in (True, False):
                c = step if fwd else N_CHUNK - 1 - step
                d = 0 if fwd else 1
                q, v, z, lb = operands(fwd, c)
                a, k, tot = decay_terms(fwd, z, lb)
                a_ref[...] = a
                k_ref[...] = k
                for h in range(HG_HEADS):
                    ln = slice(h * HG_DK, (h + 1) * HG_DK)
                    q_h = q[:, ln]
                    a_h = a[:, ln]

                    def body(s, acc, ln=ln, q_h=q_h, a_h=a_h, fwd=fwd):
                        base = pl.multiple_of((s // 8) * 8, 8)
                        pick = sub8 == (s % 8)
                        a_s = jnp.sum(jnp.where(pick, a_ref[pl.ds(base, 8), ln], 0.0),
                                      axis=0, keepdims=True)
                        k_s = jnp.sum(jnp.where(pick, k_ref[pl.ds(base, 8), ln], 0.0),
                                      axis=0, keepdims=True)
                        valid = (t_col >= s) if fwd else (t_col <= s)
                        w = jnp.exp(jnp.where(valid, a_h - a_s, 0.0))
                        col = jnp.sum(jnp.where(valid, q_h * w * k_s, 0.0), axis=1, keepdims=True)
                        return acc + col * (s_row == s).astype(F32)

                    p_ref[(step * 2 + d) * HG_HEADS + h] = lax.fori_loop(
                        0, CHUNK, body, jnp.zeros((CHUNK, CHUNK), F32))

    for step in range(N_CHUNK):
        for fwd in (True, False):
            c = step if fwd else N_CHUNK - 1 - step
            d = 0 if fwd else 1
            o_ref = of_ref if fwd else ob_ref
            _, v, _, _ = operands(fwd, c)
            v_bf = v.astype(BF16)
            for h in range(HG_HEADS):
                ln = slice(h * HG_DK, (h + 1) * HG_DK)
                p = p_ref[(step * 2 + d) * HG_HEADS + h].astype(BF16)
                rows = slice(c * CHUNK, (c + 1) * CHUNK)
                o_ref[rows, ln] = o_ref[rows, ln] + _dot(p, v_bf[:, ln])


def _scan_call(z, lb_raw, layer):
    n_batch, n_tok, _ = z.shape
    n_tiles = n_tok // TILE

    def bwd_tile(j):
        return jnp.where(j == 0, 0, n_tiles - j)

    return pl.pallas_call(
        functools.partial(_scan_kernel, layer),
        grid=(n_batch, n_tiles),
        in_specs=[
            pl.BlockSpec((None, TILE, 3 * HG_WIDTH), lambda b, j: (b, j, 0)),
            pl.BlockSpec((None, TILE, 2 * HG_WIDTH), lambda b, j: (b, bwd_tile(j), 0)),
            pl.BlockSpec((None, TILE, HG_WIDTH), lambda b, j: (b, bwd_tile(j), 3)),
            pl.BlockSpec((2 * DEPTH, HG_WIDTH), lambda b, j: (0, 0)),
        ],
        out_specs=[
            pl.BlockSpec((None, TILE, HG_WIDTH), lambda b, j: (b, j, 0)),
            pl.BlockSpec((None, TILE, HG_WIDTH), lambda b, j: (b, bwd_tile(j), 0)),
        ],
        out_shape=[jax.ShapeDtypeStruct((n_batch, n_tok, HG_WIDTH), F32)] * 2,
        scratch_shapes=[
            pltpu.VMEM((2 * HG_HEADS, HG_DK, HG_DK), F32),
            pltpu.VMEM((2, HG_WIDTH), F32),
            pltpu.VMEM((N_CHUNK * 2 * HG_HEADS, CHUNK, CHUNK), F32),
            pltpu.VMEM((CHUNK, HG_WIDTH), F32),
            pltpu.VMEM((CHUNK, HG_WIDTH), F32),
        ],
        compiler_params=pltpu.CompilerParams(
            dimension_semantics=("arbitrary", "arbitrary"), vmem_limit_bytes=VMEM_LIMIT),
        name="hgrn2_scan",
    )(z, z, z, lb_raw.reshape(2 * DEPTH, HG_WIDTH))


def _tail_kernel(final, t0, x_ref, of_ref, ob_ref, zg_ref, zuv_ref, zhb_ref, zc_ref, mod_ref,
                 n2_ref, fin_ref, hg_ref, ws_ref, bias_ref, grp_ref, cw_ref,
                 wout_ref, w1_ref, w2_ref, out_ref):
    is_ctx = (pl.program_id(1) + t0) == 0

    def mod(i):
        return mod_ref[:, i * D_MODEL:(i + 1) * D_MODEL]

    o = of_ref[...] + ob_ref[...]
    heads = []
    for h in range(HG_HEADS):
        oh = o[:, h * HG_DK:(h + 1) * HG_DK]
        heads.append(oh * lax.rsqrt(jnp.mean(oh * oh, axis=-1, keepdims=True) + EPS))
    o = jnp.concatenate(heads, axis=-1) * hg_ref[...] * _silu(zg_ref[...])

    uf = _gelu(zuv_ref[:, 0:MLP_WIDTH])
    vf = _gelu(zuv_ref[:, MLP_WIDTH:2 * MLP_WIDTH])
    sq = vf * vf
    sq_hi = sq.astype(BF16)
    sq_lo = (sq - sq_hi.astype(F32)).astype(BF16)
    ms = _dot(sq_hi, grp_ref[...]) + _dot(sq_lo, grp_ref[...])
    vn = (vf * lax.rsqrt(ms + EPS)).astype(BF16)
    lane_grp = lax.broadcasted_iota(jnp.int32, (TOK_CHUNK, MLP_WIDTH), 1) // MLP_CH
    mixed = []
    for cc in range(TILE // TOK_CHUNK):
        rows = slice(cc * TOK_CHUNK, (cc + 1) * TOK_CHUNK)
        full = _dot(ws_ref[...], vn[rows, :])
        s = bias_ref[...]
        for g in range(MLP_GROUPS):
            s = s + jnp.where(lane_grp == g, full[g * TOK_CHUNK:(g + 1) * TOK_CHUNK, :], 0.0)
        mixed.append(uf[rows, :] * s)
    m = jnp.concatenate(mixed, axis=0)

    r = lax.broadcasted_iota(jnp.int32, (TILE, 1), 0)
    col = jnp.where(is_ctx, r, r % GRID_W)
    prev_ok = col != 0
    next_ok = col != jnp.where(is_ctx, TILE - 1, GRID_W - 1)
    y = zc_ref[...] * zhb_ref[:, 0:CONV_WIDTH]
    y_prev = jnp.where(prev_ok, pltpu.roll(y, 1, axis=0), 0.0)
    y_next = jnp.where(next_ok, pltpu.roll(y, TILE - 1, axis=0), 0.0)
    cv = zhb_ref[:, CONV_WIDTH:2 * CONV_WIDTH] * (
        cw_ref[0:1, :] * y_prev + cw_ref[1:2, :] * y + cw_ref[2:3, :] * y_next)

    mix = jnp.concatenate([o, m, cv], axis=-1).astype(BF16)
    x1 = x_ref[...] + mod(2) * _dot(mix, wout_ref[...])

    h2 = (_rms(x1, n2_ref[...]) * (1.0 + mod(4)) + mod(3)).astype(BF16)
    acc = jnp.zeros((TILE, D_MODEL), F32)
    for f in range(D_FF // FF_CHUNK):
        cols = slice(f * FF_CHUNK, (f + 1) * FF_CHUNK)
        a = jnp.maximum(_dot(h2, w1_ref[:, cols]), 0.0)
        acc = acc + _dot((a * a).astype(BF16), w2_ref[cols, :])
    x2 = x1 + mod(5) * acc
    if final:
        x2 = _rms(x2, fin_ref[...])
    out_ref[...] = x2


def _tail_call(xall, o_f, o_b, z, mods_l, n2, fin, hg, ws, bias, grp, cw, wout, w1, w2, final):
    n_batch, n_tok, _ = xall.shape
    t0 = 1 if final else 0
    n_tiles = n_tok // TILE - t0
    const = dict(pipeline_mode=pl.Buffered(1))

    def tile(width, col_blk):
        return pl.BlockSpec((None, TILE, width), lambda b, t: (b, t + t0, col_blk))

    def whole(shape):
        return pl.BlockSpec(shape, lambda b, t: (0,) * len(shape), **const)

    out_tok = n_tok - t0 * TILE
    return pl.pallas_call(
        functools.partial(_tail_kernel, final, t0),
        grid=(n_batch, n_tiles),
        in_specs=[
            tile(D_MODEL, 0),
            tile(HG_WIDTH, 0),
            tile(HG_WIDTH, 0),
            tile(HG_WIDTH, 4),
            tile(2 * MLP_WIDTH, 5),
            tile(2 * CONV_WIDTH, 6),
            tile(CONV_WIDTH, 14),
            pl.BlockSpec((None, 1, N_MOD * D_MODEL),
                         lambda b, t: (jnp.where(t + t0 == 0, n_batch, b), 0, 0)),
            whole((1, D_MODEL)),
            whole((1, D_MODEL)),
            whole((1, HG_WIDTH)),
            whole((MLP_GROUPS * TOK_CHUNK, TOK_CHUNK)),
            whole((TOK_CHUNK, MLP_WIDTH)),
            whole((MLP_WIDTH, MLP_WIDTH)),
            whole((8, CONV_WIDTH)),
            whole((D_MODEL, D_MODEL)),
            whole((D_MODEL, D_FF)),
            whole((D_FF, D_MODEL)),
        ],
        out_specs=pl.BlockSpec((None, TILE, D_MODEL), lambda b, t: (b, t, 0)),
        out_shape=jax.ShapeDtypeStruct((n_batch, out_tok, D_MODEL), F32),
        compiler_params=pltpu.CompilerParams(
            dimension_semantics=("arbitrary", "arbitrary"), vmem_limit_bytes=VMEM_LIMIT),
        name="mixer_tail",
    )(xall, o_f, o_b, z, z, z, z, mods_l, n2, fin, hg, ws, bias, grp, cw, wout, w1, w2)


def kernel(x, c, ctx, c_ctx, norm1, norm2, ada_w, ada_b, w_in, lb_raw, hg_gain, tok_ws, tok_bs,
           conv_w, w_out, w1, w2, final_norm):
    n_batch = x.shape[0]
    assert x.shape[1:] == (4096, D_MODEL) and ctx.shape[1:] == (CTX_LEN, D_MODEL)
    assert n_batch + 1 <= COND_ROWS and CTX_LEN == TILE

    cond = jnp.zeros((COND_ROWS, D_MODEL), F32).at[:n_batch].set(c).at[n_batch].set(c_ctx)
    mods = _ada_call(cond, ada_w, ada_b)
    xall = jnp.concatenate([ctx, x], axis=1)

    grp = jnp.arange(MLP_WIDTH) // MLP_CH
    grp_mean = jnp.where(grp[:, None] == grp[None, :], 1.0 / MLP_CH, 0.0).astype(BF16)
    fin = final_norm.reshape(1, D_MODEL)

    for l in range(DEPTH):
        final = l == DEPTH - 1
        mods_l = mods[l].reshape(COND_ROWS, 1, N_MOD * D_MODEL)
        z = _inproj_call(xall, mods_l, norm1[l].reshape(1, D_MODEL), w_in[l].astype(BF16), 0)
        o_f, o_b = _scan_call(z, lb_raw, l)
        ws = tok_ws[l].reshape(MLP_GROUPS * TOK_CHUNK, TOK_CHUNK).astype(BF16)
        bias = jnp.repeat(tok_bs[l].T, MLP_CH, axis=1)
        cw = jnp.zeros((8, CONV_WIDTH), F32).at[:3].set(conv_w[l])
        xall = _tail_call(xall, o_f, o_b, z, mods_l, norm2[l].reshape(1, D_MODEL), fin,
                          hg_gain[l].reshape(1, HG_WIDTH), ws, bias, grp_mean, cw,
                          w_out[l].astype(BF16), w1[l].astype(BF16), w2[l].astype(BF16), final)
    return xall
```

```python
import functools
import math

import jax
import jax.numpy as jnp
from jax import lax
from jax.experimental import pallas as pl
from jax.experimental.pallas import tpu as pltpu

D_MODEL = 1024
DEPTH = 4
CTX_LEN = 256
GRID_W = 64
HG_WIDTH = 512
HG_HEADS = 4
HG_DK = 128
MLP_WIDTH = 256
MLP_GROUPS = 4
MLP_CH = 64
TOK_CHUNK = 128
CONV_WIDTH = 256
D_FF = 4096
N_MOD = 6
EPS = 1e-6
IN_COLS = 3840

TILE = 256
CHUNK = 64
N_CHUNK = TILE // CHUNK
MID = CHUNK // 2
SAFE_EXP = 64.0
COND_ROWS = 16
CONV_K_PAD = 8
ADA_TN = 1536
FF_CHUNK = 1024
VMEM_LIMIT = 56 * 1024 * 1024

F32 = jnp.float32
BF16 = jnp.bfloat16


def _dot(a, b):
    return jnp.dot(a, b, preferred_element_type=F32)


def _dot_nt(a, b):
    return lax.dot_general(a, b, (((1,), (1,)), ((), ())), preferred_element_type=F32)


def _dot_tn(a, b):
    return lax.dot_general(a, b, (((0,), (0,)), ((), ())), preferred_element_type=F32)


def _rms(x, gain):
    return x * lax.rsqrt(jnp.mean(x * x, axis=-1, keepdims=True) + EPS) * gain


def _gelu(x):
    c = math.sqrt(2.0 / math.pi)
    return x * (0.5 * (1.0 + jnp.tanh(c * (x + 0.044715 * (x * x * x)))))


def _silu(x):
    return x / (1.0 + jnp.exp(-x))


def _compiler_params():
    return pltpu.CompilerParams(
        dimension_semantics=("arbitrary", "arbitrary"), vmem_limit_bytes=VMEM_LIMIT)


def _ada_kernel(cond_ref, w_ref, b_ref, out_ref):
    s = _silu(cond_ref[...]).astype(BF16)
    out_ref[...] = _dot(s, w_ref[...].astype(BF16)) + b_ref[...]


def _ada_call(cond, ada_w, ada_b):
    n_cols = N_MOD * D_MODEL
    return pl.pallas_call(
        _ada_kernel,
        grid=(DEPTH, n_cols // ADA_TN),
        in_specs=[
            pl.BlockSpec((COND_ROWS, D_MODEL), lambda l, n: (0, 0)),
            pl.BlockSpec((None, D_MODEL, ADA_TN), lambda l, n: (l, 0, n)),
            pl.BlockSpec((None, 1, ADA_TN), lambda l, n: (l, 0, n)),
        ],
        out_specs=pl.BlockSpec((None, COND_ROWS, ADA_TN), lambda l, n: (l, 0, n)),
        out_shape=jax.ShapeDtypeStruct((DEPTH, COND_ROWS, n_cols), F32),
        compiler_params=_compiler_params(),
        name="ada_mod",
    )(cond, ada_w, ada_b.reshape(DEPTH, 1, n_cols)).reshape(DEPTH, COND_ROWS, 1, n_cols)


def _token_specs(layer, n_batch, t0):
    off = 1 if layer == 0 else 0
    ctx_spec = pl.BlockSpec((None, TILE, D_MODEL), lambda b, t: (b, 0, 0))
    lat_spec = pl.BlockSpec((None, TILE, D_MODEL),
                            lambda b, t: (b, jnp.maximum(t + t0 - off, 1 - off), 0))
    mod_spec = pl.BlockSpec((None, None, 1, N_MOD * D_MODEL),
                            lambda b, t: (layer, jnp.where(t + t0 == 0, n_batch, b), 0, 0))
    return ctx_spec, lat_spec, mod_spec


def _layer_spec(layer, shape):
    return pl.BlockSpec((None,) + shape, lambda b, t: (layer,) + (0,) * len(shape),
                        pipeline_mode=pl.Buffered(1))


def _inproj_kernel(xc_ref, xl_ref, mod_ref, gain_ref, w_ref, z_ref):
    x = jnp.where(pl.program_id(1) == 0, xc_ref[...], xl_ref[...])
    shift = mod_ref[:, 0:D_MODEL]
    scale = mod_ref[:, D_MODEL:2 * D_MODEL]
    h = _rms(x, gain_ref[...]) * (1.0 + scale) + shift
    z_ref[...] = _dot(h.astype(BF16), w_ref[...])


def _inproj_call(layer, x_ctx, x_lat, mods, norm1, w_in_bf16):
    n_batch = x_lat.shape[0]
    n_tiles = (CTX_LEN + 4096) // TILE
    ctx_spec, lat_spec, mod_spec = _token_specs(layer, n_batch, 0)
    return pl.pallas_call(
        _inproj_kernel,
        grid=(n_batch, n_tiles),
        in_specs=[ctx_spec, lat_spec, mod_spec,
                  _layer_spec(layer, (1, D_MODEL)),
                  _layer_spec(layer, (D_MODEL, IN_COLS))],
        out_specs=pl.BlockSpec((None, TILE, IN_COLS), lambda b, t: (b, t, 0)),
        out_shape=jax.ShapeDtypeStruct((n_batch, n_tiles * TILE, IN_COLS), F32),
        compiler_params=_compiler_params(),
        name="in_proj",
    )(x_ctx, x_lat, mods, norm1, w_in_bf16)


def _tri(fwd):
    t = lax.broadcasted_iota(jnp.int32, (CHUNK, CHUNK), 0)
    s = lax.broadcasted_iota(jnp.int32, (CHUNK, CHUNK), 1)
    return (s <= t) if fwd else (s >= t)


def _decay_prep(z, lb, fwd):
    k = (1.0 - lb) / (1.0 + jnp.exp(z))
    lf = jnp.log(1.0 - k)
    hi = lf.astype(BF16)
    r1 = lf - hi.astype(F32)
    mid = r1.astype(BF16)
    lo = (r1 - mid.astype(F32)).astype(BF16)
    tri = _tri(fwd).astype(BF16)
    return _dot(tri, hi) + _dot(tri, mid) + _dot(tri, lo), k


def _exact_scores(q, a, a_ref, k_ref, d, row0, ln, fwd):
    t_col = lax.broadcasted_iota(jnp.int32, (CHUNK, 1), 0)
    s_row = lax.broadcasted_iota(jnp.int32, (1, CHUNK), 1)
    sub8 = lax.broadcasted_iota(jnp.int32, (8, 1), 0)

    def body(s, acc):
        base = pl.multiple_of(row0 + (s // 8) * 8, 8)
        pick = sub8 == (s % 8)
        a_s = jnp.sum(jnp.where(pick, a_ref[d, pl.ds(base, 8), ln], 0.0), axis=0, keepdims=True)
        k_s = jnp.sum(jnp.where(pick, k_ref[d, pl.ds(base, 8), ln], 0.0), axis=0, keepdims=True)
        valid = (t_col >= s) if fwd else (t_col <= s)
        w = jnp.exp(jnp.where(valid, a - a_s, 0.0))
        col = jnp.sum(jnp.where(valid, q * w * k_s, 0.0), axis=1, keepdims=True)
        return acc + col * (s_row == s).astype(F32)

    return lax.fori_loop(0, CHUNK, body, jnp.zeros((CHUNK, CHUNK), F32))


def _scan_kernel(layer, zf_ref, zbqi_ref, zbfb_ref, lbraw_ref, of_ref, ob_ref,
                 st_ref, lb_ref, a_ref, k_ref):
    @pl.when(pl.program_id(1) == 0)
    def _():
        st_ref[...] = jnp.zeros_like(st_ref)
        for d in range(2):
            rows = [lbraw_ref[d * DEPTH + i:d * DEPTH + i + 1, :] for i in range(DEPTH)]
            top = functools.reduce(jnp.maximum, rows)
            ex = [jnp.exp(row - top) for row in rows]
            part = functools.reduce(lambda u, w: u + w, ex[1:layer + 1], jnp.zeros_like(top))
            lb_ref[d:d + 1, :] = part / functools.reduce(lambda u, w: u + w, ex)

    worst = jnp.zeros((1, HG_WIDTH), F32)
    for c in range(N_CHUNK):
        rows = slice(c * CHUNK, (c + 1) * CHUNK)
        for d, fwd in enumerate((True, False)):
            z = zf_ref[rows, 2 * HG_WIDTH:3 * HG_WIDTH] if fwd else zbfb_ref[rows, :]
            a, k = _decay_prep(z, lb_ref[d:d + 1, :], fwd)
            a_ref[d, rows, :] = a
            k_ref[d, rows, :] = k
            ref_row = a[MID:MID + 1, :]
            worst = jnp.maximum(worst, jnp.maximum(jnp.abs(a[0:1, :] - ref_row),
                                                   jnp.abs(a[CHUNK - 1:CHUNK, :] - ref_row)))
    safe = jnp.max(worst) <= SAFE_EXP

    units = [(d, h) for d in range(2) for h in range(HG_HEADS)]

    def load_step(step, factorised):
        out = []
        for d, h in units:
            fwd = d == 0
            c = step if fwd else N_CHUNK - 1 - step
            rows = slice(c * CHUNK, (c + 1) * CHUNK)
            ln = slice(h * HG_DK, (h + 1) * HG_DK)
            qv_ref = zf_ref if fwd else zbqi_ref
            q = qv_ref[rows, h * HG_DK:(h + 1) * HG_DK]
            v = qv_ref[rows, HG_WIDTH + h * HG_DK:HG_WIDTH + (h + 1) * HG_DK].astype(BF16)
            a = a_ref[d, rows, ln]
            k = k_ref[d, rows, ln]
            tot = a[CHUNK - 1:CHUNK, :] if fwd else a[0:1, :]
            if factorised:
                ref_row = a[MID:MID + 1, :]
                e_mid = jnp.exp(a - ref_row)
                q_mid = q * e_mid
                k_mid = k / e_mid
                q_in = q_mid * jnp.exp(ref_row)
                k_end = k_mid * jnp.exp(tot - ref_row)
                p = (q_mid.astype(BF16), k_mid.astype(BF16))
            else:
                q_in = q * jnp.exp(a)
                k_end = k * jnp.exp(tot - a)
                p = _exact_scores(q, a, a_ref, k_ref, d, c * CHUNK, ln, fwd)
            out.append((q_in.astype(BF16), k_end.astype(BF16), v, jnp.exp(tot), p))
        return out

    def chunk_units(factorised):
        states = [st_ref[i] for i in range(len(units))]
        ops = load_step(0, factorised)
        for step in range(N_CHUNK):
            scores, outs = [], []
            for i, (d, h) in enumerate(units):
                p = ops[i][4]
                if factorised:
                    p = jnp.where(_tri(d == 0), _dot_nt(p[0], p[1]), 0.0)
                scores.append(p.astype(BF16))
            for i in range(len(units)):
                q_in, k_end, v, dec, _ = ops[i]
                outs.append(_dot_nt(q_in, states[i].astype(BF16)))
                states[i] = states[i] * dec + _dot_tn(v, k_end)
            for i in range(len(units)):
                outs[i] = outs[i] + _dot(scores[i], ops[i][2])
            if step + 1 < N_CHUNK:
                ops = load_step(step + 1, factorised)
            for i, (d, h) in enumerate(units):
                c = step if d == 0 else N_CHUNK - 1 - step
                o_ref = of_ref if d == 0 else ob_ref
                o_ref[c * CHUNK:(c + 1) * CHUNK, h * HG_DK:(h + 1) * HG_DK] = outs[i]
        for i in range(len(units)):
            st_ref[i] = states[i]

    @pl.when(safe)
    def _():
        chunk_units(True)

    @pl.when(jnp.logical_not(safe))
    def _():
        chunk_units(False)


def _scan_call(z, lb_raw, layer):
    n_batch, n_tok, _ = z.shape
    n_tiles = n_tok // TILE

    def bwd_tile(j):
        return jnp.where(j == 0, 0, n_tiles - j)

    return pl.pallas_call(
        functools.partial(_scan_kernel, layer),
        grid=(n_batch, n_tiles),
        in_specs=[
            pl.BlockSpec((None, TILE, 3 * HG_WIDTH), lambda b, j: (b, j, 0)),
            pl.BlockSpec((None, TILE, 2 * HG_WIDTH), lambda b, j: (b, bwd_tile(j), 0)),
            pl.BlockSpec((None, TILE, HG_WIDTH), lambda b, j: (b, bwd_tile(j), 3)),
            pl.BlockSpec((2 * DEPTH, HG_WIDTH), lambda b, j: (0, 0)),
        ],
        out_specs=[
            pl.BlockSpec((None, TILE, HG_WIDTH), lambda b, j: (b, j, 0)),
            pl.BlockSpec((None, TILE, HG_WIDTH), lambda b, j: (b, bwd_tile(j), 0)),
        ],
        out_shape=[jax.ShapeDtypeStruct((n_batch, n_tok, HG_WIDTH), F32)] * 2,
        scratch_shapes=[
            pltpu.VMEM((2 * HG_HEADS, HG_DK, HG_DK), F32),
            pltpu.VMEM((2, HG_WIDTH), F32),
            pltpu.VMEM((2, TILE, HG_WIDTH), F32),
            pltpu.VMEM((2, TILE, HG_WIDTH), F32),
        ],
        compiler_params=_compiler_params(),
        name="hgrn2_scan",
    )(z, z, z, lb_raw.reshape(2 * DEPTH, HG_WIDTH))


def _tail_kernel(final, t0, xc_ref, xl_ref, of_ref, ob_ref, zg_ref, zuv_ref, zhb_ref, zc_ref,
                 mod_ref, n2_ref, fin_ref, hg_ref, ws_ref, bs_ref, grp_ref, cw_ref,
                 wout_ref, w1_ref, w2_ref, out_ref):
    is_ctx = (pl.program_id(1) + t0) == 0
    x = jnp.where(is_ctx, xc_ref[...], xl_ref[...])

    def mod(i):
        return mod_ref[:, i * D_MODEL:(i + 1) * D_MODEL]

    o = of_ref[...] + ob_ref[...]
    heads = []
    for h in range(HG_HEADS):
        oh = o[:, h * HG_DK:(h + 1) * HG_DK]
        heads.append(oh * lax.rsqrt(jnp.mean(oh * oh, axis=-1, keepdims=True) + EPS))
    o = jnp.concatenate(heads, axis=-1) * hg_ref[...] * _silu(zg_ref[...])

    uf = _gelu(zuv_ref[:, 0:MLP_WIDTH])
    vf = _gelu(zuv_ref[:, MLP_WIDTH:2 * MLP_WIDTH])
    sq = vf * vf
    sq_hi = sq.astype(BF16)
    sq_lo = (sq - sq_hi.astype(F32)).astype(BF16)
    ms = _dot(sq_hi, grp_ref[...]) + _dot(sq_lo, grp_ref[...])
    vn = (vf * lax.rsqrt(ms + EPS)).astype(BF16)
    lane_grp = lax.broadcasted_iota(jnp.int32, (TOK_CHUNK, MLP_WIDTH), 1) // MLP_CH
    ws = ws_ref[...].astype(BF16)
    mixed = []
    for cc in range(TILE // TOK_CHUNK):
        rows = slice(cc * TOK_CHUNK, (cc + 1) * TOK_CHUNK)
        full = _dot(ws, vn[rows, :])
        s = bs_ref[...]
        for g in range(MLP_GROUPS):
            s = s + jnp.where(lane_grp == g, full[g * TOK_CHUNK:(g + 1) * TOK_CHUNK, :], 0.0)
        mixed.append(uf[rows, :] * s)
    m = jnp.concatenate(mixed, axis=0)

    r = lax.broadcasted_iota(jnp.int32, (TILE, 1), 0)
    col = jnp.where(is_ctx, r, r % GRID_W)
    prev_ok = col != 0
    next_ok = col != jnp.where(is_ctx, TILE - 1, GRID_W - 1)
    y = zc_ref[...] * zhb_ref[:, 0:CONV_WIDTH]
    y_prev = jnp.where(prev_ok, pltpu.roll(y, 1, axis=0), 0.0)
    y_next = jnp.where(next_ok, pltpu.roll(y, TILE - 1, axis=0), 0.0)
    cv = zhb_ref[:, CONV_WIDTH:2 * CONV_WIDTH] * (
        cw_ref[0:1, :] * y_prev + cw_ref[1:2, :] * y + cw_ref[2:3, :] * y_next)

    mix = jnp.concatenate([o, m, cv], axis=-1).astype(BF16)
    x1 = x + mod(2) * _dot(mix, wout_ref[...])

    h2 = (_rms(x1, n2_ref[...]) * (1.0 + mod(4)) + mod(3)).astype(BF16)
    acc = jnp.zeros((TILE, D_MODEL), F32)
    for f in range(D_FF // FF_CHUNK):
        cols = slice(f * FF_CHUNK, (f + 1) * FF_CHUNK)
        a = jnp.maximum(_dot(h2, w1_ref[:, cols]), 0.0)
        acc = acc + _dot((a * a).astype(BF16), w2_ref[cols, :])
    x2 = x1 + mod(5) * acc
    if final:
        x2 = _rms(x2, fin_ref[...])
    out_ref[...] = x2


def _tail_call(layer, x_ctx, x_lat, o_f, o_b, z, mods, norm2, fin, hg_gain, tok_ws, bias, grp,
               conv_w, wout, w1, w2):
    n_batch = x_lat.shape[0]
    final = layer == DEPTH - 1
    t0 = 1 if final else 0
    n_tiles = (CTX_LEN + 4096) // TILE - t0
    ctx_spec, lat_spec, mod_spec = _token_specs(layer, n_batch, t0)

    def tile(width, col_blk):
        return pl.BlockSpec((None, TILE, width), lambda b, t: (b, t + t0, col_blk))

    return pl.pallas_call(
        functools.partial(_tail_kernel, final, t0),
        grid=(n_batch, n_tiles),
        in_specs=[
            ctx_spec, lat_spec,
            tile(HG_WIDTH, 0),
            tile(HG_WIDTH, 0),
            tile(HG_WIDTH, 4),
            tile(2 * MLP_WIDTH, 5),
            tile(2 * CONV_WIDTH, 6),
            tile(CONV_WIDTH, 14),
            mod_spec,
            _layer_spec(layer, (1, D_MODEL)),
            pl.BlockSpec((1, D_MODEL), lambda b, t: (0, 0)),
            _layer_spec(layer, (1, HG_WIDTH)),
            _layer_spec(layer, (MLP_GROUPS * TOK_CHUNK, TOK_CHUNK)),
            _layer_spec(layer, (TOK_CHUNK, MLP_WIDTH)),
            pl.BlockSpec((MLP_WIDTH, MLP_WIDTH), lambda b, t: (0, 0)),
            _layer_spec(layer, (CONV_K_PAD, CONV_WIDTH)),
            _layer_spec(layer, (D_MODEL, D_MODEL)),
            _layer_spec(layer, (D_MODEL, D_FF)),
            _layer_spec(layer, (D_FF, D_MODEL)),
        ],
        out_specs=pl.BlockSpec((None, TILE, D_MODEL), lambda b, t: (b, t, 0)),
        out_shape=jax.ShapeDtypeStruct((n_batch, n_tiles * TILE, D_MODEL), F32),
        compiler_params=_compiler_params(),
        name="mixer_tail",
    )(x_ctx, x_lat, o_f, o_b, z, z, z, z, mods, norm2, fin, hg_gain, tok_ws, bias, grp, conv_w,
      wout, w1, w2)


def kernel(x, c, ctx, c_ctx, norm1, norm2, ada_w, ada_b, w_in, lb_raw, hg_gain, tok_ws, tok_bs,
           conv_w, w_out, w1, w2, final_norm):
    n_batch = x.shape[0]
    assert x.shape[1:] == (4096, D_MODEL) and ctx.shape[1:] == (CTX_LEN, D_MODEL)
    assert n_batch + 1 <= COND_ROWS and CTX_LEN == TILE

    cond = jnp.zeros((COND_ROWS, D_MODEL), F32).at[:n_batch].set(c).at[n_batch].set(c_ctx)
    mods = _ada_call(cond, ada_w, ada_b)

    grp = jnp.arange(MLP_WIDTH) // MLP_CH
    grp_mean = jnp.where(grp[:, None] == grp[None, :], 1.0 / MLP_CH, 0.0).astype(BF16)
    bias = jnp.repeat(jnp.swapaxes(tok_bs, 1, 2), MLP_CH, axis=2)
    ws = tok_ws.reshape(DEPTH, MLP_GROUPS * TOK_CHUNK, TOK_CHUNK)
    cw = jnp.zeros((DEPTH, CONV_K_PAD, CONV_WIDTH), F32).at[:, :3].set(conv_w)
    n1 = norm1.reshape(DEPTH, 1, D_MODEL)
    n2 = norm2.reshape(DEPTH, 1, D_MODEL)
    hg = hg_gain.reshape(DEPTH, 1, HG_WIDTH)
    fin = final_norm.reshape(1, D_MODEL)
    w_in_b, w_out_b, w1_b, w2_b = (w.astype(BF16) for w in (w_in, w_out, w1, w2))

    x_ctx, x_lat = ctx, x
    for l in range(DEPTH):
        z = _inproj_call(l, x_ctx, x_lat, mods, n1, w_in_b)
        o_f, o_b = _scan_call(z, lb_raw, l)
        x_ctx = x_lat = _tail_call(l, x_ctx, x_lat, o_f, o_b, z, mods, n2, fin, hg, ws, bias,
                                   grp_mean, cw, w_out_b, w1_b, w2_b)
    return x_lat
```

```python
import functools
import math

import jax
import jax.numpy as jnp
from jax import lax
from jax.experimental import pallas as pl
from jax.experimental.pallas import tpu as pltpu

D_MODEL = 1024
DEPTH = 4
CTX_LEN = 256
GRID_W = 64
HG_WIDTH = 512
HG_HEADS = 4
HG_DK = 128
MLP_WIDTH = 256
MLP_GROUPS = 4
MLP_CH = 64
TOK_CHUNK = 128
CONV_WIDTH = 256
D_FF = 4096
N_MOD = 6
EPS = 1e-6
IN_COLS = 3840

TILE = 256
PAIR = 2
CHUNK = 64
N_CHUNK = TILE // CHUNK
MID = CHUNK // 2
SAFE_EXP = 64.0
COND_ROWS = 16
CONV_K_PAD = 8
ADA_TN = 1536
FF_CHUNK = 1024
VMEM_LIMIT = 56 * 1024 * 1024

F32 = jnp.float32
BF16 = jnp.bfloat16


def _dot(a, b):
    return jnp.dot(a, b, preferred_element_type=F32)


def _dot_nt(a, b):
    return lax.dot_general(a, b, (((1,), (1,)), ((), ())), preferred_element_type=F32)


def _dot_tn(a, b):
    return lax.dot_general(a, b, (((0,), (0,)), ((), ())), preferred_element_type=F32)


def _rms(x, gain):
    return x * lax.rsqrt(jnp.mean(x * x, axis=-1, keepdims=True) + EPS) * gain


def _gelu(x):
    c = math.sqrt(2.0 / math.pi)
    return x * (0.5 * (1.0 + jnp.tanh(c * (x + 0.044715 * (x * x * x)))))


def _silu(x):
    return x / (1.0 + jnp.exp(-x))


def _compiler_params():
    return pltpu.CompilerParams(
        dimension_semantics=("arbitrary", "arbitrary"), vmem_limit_bytes=VMEM_LIMIT)


def _ada_kernel(cond_ref, w_ref, b_ref, out_ref):
    s = _silu(cond_ref[...]).astype(BF16)
    out_ref[...] = _dot(s, w_ref[...].astype(BF16)) + b_ref[...]


def _ada_call(cond, ada_w, ada_b):
    n_cols = N_MOD * D_MODEL
    return pl.pallas_call(
        _ada_kernel,
        grid=(DEPTH, n_cols // ADA_TN),
        in_specs=[
            pl.BlockSpec((COND_ROWS, D_MODEL), lambda l, n: (0, 0)),
            pl.BlockSpec((None, D_MODEL, ADA_TN), lambda l, n: (l, 0, n)),
            pl.BlockSpec((None, 1, ADA_TN), lambda l, n: (l, 0, n)),
        ],
        out_specs=pl.BlockSpec((None, COND_ROWS, ADA_TN), lambda l, n: (l, 0, n)),
        out_shape=jax.ShapeDtypeStruct((DEPTH, COND_ROWS, n_cols), F32),
        compiler_params=_compiler_params(),
        name="ada_mod",
    )(cond, ada_w, ada_b.reshape(DEPTH, 1, n_cols)).reshape(DEPTH, COND_ROWS, 1, n_cols)


def _token_specs(layer, n_batch, t0):
    off = 1 if layer == 0 else 0
    ctx_spec = pl.BlockSpec((PAIR, TILE, D_MODEL), lambda b, t: (b, 0, 0))
    lat_spec = pl.BlockSpec((PAIR, TILE, D_MODEL),
                            lambda b, t: (b, jnp.maximum(t + t0 - off, 1 - off), 0))
    mod_spec = pl.BlockSpec(
        (None, PAIR, 1, N_MOD * D_MODEL),
        lambda b, t: (layer, jnp.where(t + t0 == 0, n_batch // PAIR, b), 0, 0))
    return ctx_spec, lat_spec, mod_spec


def _layer_spec(layer, shape):
    return pl.BlockSpec((None,) + shape, lambda b, t: (layer,) + (0,) * len(shape),
                        pipeline_mode=pl.Buffered(1))


def _inproj_kernel(xc_ref, xl_ref, mod_ref, gain_ref, w_ref, z_ref):
    is_ctx = pl.program_id(1) == 0
    for p in range(PAIR):
        x = jnp.where(is_ctx, xc_ref[p], xl_ref[p])
        shift = mod_ref[p, :, 0:D_MODEL]
        scale = mod_ref[p, :, D_MODEL:2 * D_MODEL]
        h = _rms(x, gain_ref[...]) * (1.0 + scale) + shift
        z_ref[p] = _dot(h.astype(BF16), w_ref[...])


def _inproj_call(layer, x_ctx, x_lat, mods, norm1, w_in_bf16):
    n_batch = x_lat.shape[0]
    n_tiles = (CTX_LEN + 4096) // TILE
    ctx_spec, lat_spec, mod_spec = _token_specs(layer, n_batch, 0)
    return pl.pallas_call(
        _inproj_kernel,
        grid=(n_batch // PAIR, n_tiles),
        in_specs=[ctx_spec, lat_spec, mod_spec,
                  _layer_spec(layer, (1, D_MODEL)),
                  _layer_spec(layer, (D_MODEL, IN_COLS))],
        out_specs=pl.BlockSpec((PAIR, TILE, IN_COLS), lambda b, t: (b, t, 0)),
        out_shape=jax.ShapeDtypeStruct((n_batch, n_tiles * TILE, IN_COLS), F32),
        compiler_params=_compiler_params(),
        name="in_proj",
    )(x_ctx, x_lat, mods, norm1, w_in_bf16)


def _tri(fwd):
    t = lax.broadcasted_iota(jnp.int32, (CHUNK, CHUNK), 0)
    s = lax.broadcasted_iota(jnp.int32, (CHUNK, CHUNK), 1)
    return (s <= t) if fwd else (s >= t)


def _decay_prep(z, lb, fwd):
    k = (1.0 - lb) / (1.0 + jnp.exp(z))
    lf = jnp.log(1.0 - k)
    hi = lf.astype(BF16)
    r1 = lf - hi.astype(F32)
    mid = r1.astype(BF16)
    lo = (r1 - mid.astype(F32)).astype(BF16)
    tri = _tri(fwd).astype(BF16)
    return _dot(tri, hi) + _dot(tri, mid) + _dot(tri, lo), k


def _exact_scores(q, a, a_ref, k_ref, d, row0, ln, fwd):
    t_col = lax.broadcasted_iota(jnp.int32, (CHUNK, 1), 0)
    s_row = lax.broadcasted_iota(jnp.int32, (1, CHUNK), 1)
    sub8 = lax.broadcasted_iota(jnp.int32, (8, 1), 0)

    def body(s, acc):
        base = pl.multiple_of(row0 + (s // 8) * 8, 8)
        pick = sub8 == (s % 8)
        a_s = jnp.sum(jnp.where(pick, a_ref[d, pl.ds(base, 8), ln], 0.0), axis=0, keepdims=True)
        k_s = jnp.sum(jnp.where(pick, k_ref[d, pl.ds(base, 8), ln], 0.0), axis=0, keepdims=True)
        valid = (t_col >= s) if fwd else (t_col <= s)
        w = jnp.exp(jnp.where(valid, a - a_s, 0.0))
        col = jnp.sum(jnp.where(valid, q * w * k_s, 0.0), axis=1, keepdims=True)
        return acc + col * (s_row == s).astype(F32)

    return lax.fori_loop(0, CHUNK, body, jnp.zeros((CHUNK, CHUNK), F32))


def _scan_kernel(layer, zf_ref, zbqi_ref, zbfb_ref, lbraw_ref, of_ref, ob_ref,
                 st_ref, lb_ref, a_ref, k_ref):
    @pl.when(pl.program_id(1) == 0)
    def _():
        st_ref[...] = jnp.zeros_like(st_ref)
        for d in range(2):
            rows = [lbraw_ref[d * DEPTH + i:d * DEPTH + i + 1, :] for i in range(DEPTH)]
            top = functools.reduce(jnp.maximum, rows)
            ex = [jnp.exp(row - top) for row in rows]
            part = functools.reduce(lambda u, w: u + w, ex[1:layer + 1], jnp.zeros_like(top))
            lb_ref[d:d + 1, :] = part / functools.reduce(lambda u, w: u + w, ex)

    worst = jnp.zeros((1, HG_WIDTH), F32)
    for c in range(N_CHUNK):
        rows = slice(c * CHUNK, (c + 1) * CHUNK)
        for d, fwd in enumerate((True, False)):
            z = zf_ref[rows, 2 * HG_WIDTH:3 * HG_WIDTH] if fwd else zbfb_ref[rows, :]
            a, k = _decay_prep(z, lb_ref[d:d + 1, :], fwd)
            a_ref[d, rows, :] = a
            k_ref[d, rows, :] = k
            ref_row = a[MID:MID + 1, :]
            worst = jnp.maximum(worst, jnp.maximum(jnp.abs(a[0:1, :] - ref_row),
                                                   jnp.abs(a[CHUNK - 1:CHUNK, :] - ref_row)))
    safe = jnp.max(worst) <= SAFE_EXP

    units = [(d, h) for d in range(2) for h in range(HG_HEADS)]

    def load_step(step, factorised):
        out = []
        for d, h in units:
            fwd = d == 0
            c = step if fwd else N_CHUNK - 1 - step
            rows = slice(c * CHUNK, (c + 1) * CHUNK)
            ln = slice(h * HG_DK, (h + 1) * HG_DK)
            qv_ref = zf_ref if fwd else zbqi_ref
            q = qv_ref[rows, h * HG_DK:(h + 1) * HG_DK]
            v = qv_ref[rows, HG_WIDTH + h * HG_DK:HG_WIDTH + (h + 1) * HG_DK].astype(BF16)
            a = a_ref[d, rows, ln]
            k = k_ref[d, rows, ln]
            tot = a[CHUNK - 1:CHUNK, :] if fwd else a[0:1, :]
            if factorised:
                ref_row = a[MID:MID + 1, :]
                e_mid = jnp.exp(a - ref_row)
                q_mid = q * e_mid
                k_mid = k / e_mid
                q_in = q_mid * jnp.exp(ref_row)
                k_end = k_mid * jnp.exp(tot - ref_row)
                p = (q_mid.astype(BF16), k_mid.astype(BF16))
            else:
                q_in = q * jnp.exp(a)
                k_end = k * jnp.exp(tot - a)
                p = _exact_scores(q, a, a_ref, k_ref, d, c * CHUNK, ln, fwd)
            out.append((q_in.astype(BF16), k_end.astype(BF16), v, jnp.exp(tot), p))
        return out

    def chunk_units(factorised):
        states = [st_ref[i] for i in range(len(units))]
        ops = load_step(0, factorised)
        for step in range(N_CHUNK):
            scores, outs = [], []
            for i, (d, h) in enumerate(units):
                p = ops[i][4]
                if factorised:
                    p = jnp.where(_tri(d == 0), _dot_nt(p[0], p[1]), 0.0)
                scores.append(p.astype(BF16))
            for i in range(len(units)):
                q_in, k_end, v, dec, _ = ops[i]
                outs.append(_dot_nt(q_in, states[i].astype(BF16)))
                states[i] = states[i] * dec + _dot_tn(v, k_end)
            for i in range(len(units)):
                outs[i] = outs[i] + _dot(scores[i], ops[i][2])
            if step + 1 < N_CHUNK:
                ops = load_step(step + 1, factorised)
            for i, (d, h) in enumerate(units):
                c = step if d == 0 else N_CHUNK - 1 - step
                o_ref = of_ref if d == 0 else ob_ref
                o_ref[c * CHUNK:(c + 1) * CHUNK, h * HG_DK:(h + 1) * HG_DK] = outs[i]
        for i in range(len(units)):
            st_ref[i] = states[i]

    @pl.when(safe)
    def _():
        chunk_units(True)

    @pl.when(jnp.logical_not(safe))
    def _():
        chunk_units(False)


def _scan_call(z, lb_raw, layer):
    n_batch, n_tok, _ = z.shape
    n_tiles = n_tok // TILE

    def bwd_tile(j):
        return jnp.where(j == 0, 0, n_tiles - j)

    return pl.pallas_call(
        functools.partial(_scan_kernel, layer),
        grid=(n_batch, n_tiles),
        in_specs=[
            pl.BlockSpec((None, TILE, 3 * HG_WIDTH), lambda b, j: (b, j, 0)),
            pl.BlockSpec((None, TILE, 2 * HG_WIDTH), lambda b, j: (b, bwd_tile(j), 0)),
            pl.BlockSpec((None, TILE, HG_WIDTH), lambda b, j: (b, bwd_tile(j), 3)),
            pl.BlockSpec((2 * DEPTH, HG_WIDTH), lambda b, j: (0, 0)),
        ],
        out_specs=[
            pl.BlockSpec((None, TILE, HG_WIDTH), lambda b, j: (b, j, 0)),
            pl.BlockSpec((None, TILE, HG_WIDTH), lambda b, j: (b, bwd_tile(j), 0)),
        ],
        out_shape=[jax.ShapeDtypeStruct((n_batch, n_tok, HG_WIDTH), F32)] * 2,
        scratch_shapes=[
            pltpu.VMEM((2 * HG_HEADS, HG_DK, HG_DK), F32),
            pltpu.VMEM((2, HG_WIDTH), F32),
            pltpu.VMEM((2, TILE, HG_WIDTH), F32),
            pltpu.VMEM((2, TILE, HG_WIDTH), F32),
        ],
        compiler_params=_compiler_params(),
        name="hgrn2_scan",
    )(z, z, z, lb_raw.reshape(2 * DEPTH, HG_WIDTH))


def _tail_kernel(final, t0, xc_ref, xl_ref, of_ref, ob_ref, zg_ref, zuv_ref, zhb_ref, zc_ref,
                 mod_ref, n2_ref, fin_ref, hg_ref, ws_ref, bs_ref, grp_ref, cw_ref,
                 wout_ref, w1_ref, w2_ref, out_ref):
    is_ctx = (pl.program_id(1) + t0) == 0
    pair = range(PAIR)

    def mod(p, i):
        return mod_ref[p, :, i * D_MODEL:(i + 1) * D_MODEL]

    vf = [_gelu(zuv_ref[p, :, MLP_WIDTH:2 * MLP_WIDTH]) for p in pair]
    ms = []
    for p in pair:
        sq = vf[p] * vf[p]
        sq_hi = sq.astype(BF16)
        sq_lo = (sq - sq_hi.astype(F32)).astype(BF16)
        ms.append(_dot(sq_hi, grp_ref[...]) + _dot(sq_lo, grp_ref[...]))

    og = []
    for p in pair:
        o = of_ref[p] + ob_ref[p]
        heads = []
        for h in range(HG_HEADS):
            oh = o[:, h * HG_DK:(h + 1) * HG_DK]
            heads.append(oh * lax.rsqrt(jnp.mean(oh * oh, axis=-1, keepdims=True) + EPS))
        og.append(jnp.concatenate(heads, axis=-1) * hg_ref[...] * _silu(zg_ref[p]))

    r = lax.broadcasted_iota(jnp.int32, (TILE, 1), 0)
    col = jnp.where(is_ctx, r, r % GRID_W)
    prev_ok = col != 0
    next_ok = col != jnp.where(is_ctx, TILE - 1, GRID_W - 1)
    cv = []
    for p in pair:
        y = zc_ref[p] * zhb_ref[p, :, 0:CONV_WIDTH]
        y_prev = jnp.where(prev_ok, pltpu.roll(y, 1, axis=0), 0.0)
        y_next = jnp.where(next_ok, pltpu.roll(y, TILE - 1, axis=0), 0.0)
        cv.append(zhb_ref[p, :, CONV_WIDTH:2 * CONV_WIDTH] * (
            cw_ref[0:1, :] * y_prev + cw_ref[1:2, :] * y + cw_ref[2:3, :] * y_next))

    ws = ws_ref[...].astype(BF16)
    n_tok_chunk = TILE // TOK_CHUNK
    full = []
    for p in pair:
        vn = (vf[p] * lax.rsqrt(ms[p] + EPS)).astype(BF16)
        full.append([_dot(ws, vn[cc * TOK_CHUNK:(cc + 1) * TOK_CHUNK, :])
                     for cc in range(n_tok_chunk)])
    lane_grp = lax.broadcasted_iota(jnp.int32, (TOK_CHUNK, MLP_WIDTH), 1) // MLP_CH
    mix = []
    for p in pair:
        uf = _gelu(zuv_ref[p, :, 0:MLP_WIDTH])
        mixed = []
        for cc in range(n_tok_chunk):
            s = bs_ref[...]
            for g in range(MLP_GROUPS):
                s = s + jnp.where(lane_grp == g,
                                  full[p][cc][g * TOK_CHUNK:(g + 1) * TOK_CHUNK, :], 0.0)
            mixed.append(uf[cc * TOK_CHUNK:(cc + 1) * TOK_CHUNK, :] * s)
        m = jnp.concatenate(mixed, axis=0)
        mix.append(jnp.concatenate([og[p], m, cv[p]], axis=-1).astype(BF16))

    proj = [_dot(mix[p], wout_ref[...]) for p in pair]
    x1, h2 = [], []
    for p in pair:
        x = jnp.where(is_ctx, xc_ref[p], xl_ref[p])
        x1.append(x + mod(p, 2) * proj[p])
        h2.append((_rms(x1[p], n2_ref[...]) * (1.0 + mod(p, 4)) + mod(p, 3)).astype(BF16))

    jobs = [(p, f) for f in range(D_FF // FF_CHUNK) for p in pair]
    acc = [jnp.zeros((TILE, D_MODEL), F32) for _ in pair]
    pending = None
    for job in jobs + [None]:
        issued = None
        if job is not None:
            p, f = job
            a = jnp.maximum(_dot(h2[p], w1_ref[:, f * FF_CHUNK:(f + 1) * FF_CHUNK]), 0.0)
            issued = (p, f, (a * a).astype(BF16))
        if pending is not None:
            p, f, act = pending
            acc[p] = acc[p] + _dot(act, w2_ref[f * FF_CHUNK:(f + 1) * FF_CHUNK, :])
        pending = issued

    for p in pair:
        x2 = x1[p] + mod(p, 5) * acc[p]
        if final:
            x2 = _rms(x2, fin_ref[...])
        out_ref[p] = x2


def _tail_call(layer, x_ctx, x_lat, o_f, o_b, z, mods, norm2, fin, hg_gain, tok_ws, bias, grp,
               conv_w, wout, w1, w2):
    n_batch = x_lat.shape[0]
    final = layer == DEPTH - 1
    t0 = 1 if final else 0
    n_tiles = (CTX_LEN + 4096) // TILE - t0
    ctx_spec, lat_spec, mod_spec = _token_specs(layer, n_batch, t0)

    def tile(width, col_blk):
        return pl.BlockSpec((PAIR, TILE, width), lambda b, t: (b, t + t0, col_blk))

    return pl.pallas_call(
        functools.partial(_tail_kernel, final, t0),
        grid=(n_batch // PAIR, n_tiles),
        in_specs=[
            ctx_spec, lat_spec,
            tile(HG_WIDTH, 0),
            tile(HG_WIDTH, 0),
            tile(HG_WIDTH, 4),
            tile(2 * MLP_WIDTH, 5),
            tile(2 * CONV_WIDTH, 6),
            tile(CONV_WIDTH, 14),
            mod_spec,
            _layer_spec(layer, (1, D_MODEL)),
            pl.BlockSpec((1, D_MODEL), lambda b, t: (0, 0)),
            _layer_spec(layer, (1, HG_WIDTH)),
            _layer_spec(layer, (MLP_GROUPS * TOK_CHUNK, TOK_CHUNK)),
            _layer_spec(layer, (TOK_CHUNK, MLP_WIDTH)),
            pl.BlockSpec((MLP_WIDTH, MLP_WIDTH), lambda b, t: (0, 0)),
            _layer_spec(layer, (CONV_K_PAD, CONV_WIDTH)),
            _layer_spec(layer, (D_MODEL, D_MODEL)),
            _layer_spec(layer, (D_MODEL, D_FF)),
            _layer_spec(layer, (D_FF, D_MODEL)),
        ],
        out_specs=pl.BlockSpec((PAIR, TILE, D_MODEL), lambda b, t: (b, t, 0)),
        out_shape=jax.ShapeDtypeStruct((n_batch, n_tiles * TILE, D_MODEL), F32),
        compiler_params=_compiler_params(),
        name="mixer_tail",
    )(x_ctx, x_lat, o_f, o_b, z, z, z, z, mods, norm2, fin, hg_gain, tok_ws, bias, grp, conv_w,
      wout, w1, w2)


def kernel(x, c, ctx, c_ctx, norm1, norm2, ada_w, ada_b, w_in, lb_raw, hg_gain, tok_ws, tok_bs,
           conv_w, w_out, w1, w2, final_norm):
    n_batch = x.shape[0]
    assert x.shape[1:] == (4096, D_MODEL) and ctx.shape[1:] == (CTX_LEN, D_MODEL)
    assert n_batch % PAIR == 0 and n_batch + PAIR <= COND_ROWS and CTX_LEN == TILE

    cond = jnp.zeros((COND_ROWS, D_MODEL), F32).at[:n_batch].set(c)
    cond = cond.at[n_batch:n_batch + PAIR].set(jnp.broadcast_to(c_ctx, (PAIR, D_MODEL)))
    mods = _ada_call(cond, ada_w, ada_b)

    grp = jnp.arange(MLP_WIDTH) // MLP_CH
    grp_mean = jnp.where(grp[:, None] == grp[None, :], 1.0 / MLP_CH, 0.0).astype(BF16)
    bias = jnp.repeat(jnp.swapaxes(tok_bs, 1, 2), MLP_CH, axis=2)
    ws = tok_ws.reshape(DEPTH, MLP_GROUPS * TOK_CHUNK, TOK_CHUNK)
    cw = jnp.zeros((DEPTH, CONV_K_PAD, CONV_WIDTH), F32).at[:, :3].set(conv_w)
    n1 = norm1.reshape(DEPTH, 1, D_MODEL)
    n2 = norm2.reshape(DEPTH, 1, D_MODEL)
    hg = hg_gain.reshape(DEPTH, 1, HG_WIDTH)
    fin = final_norm.reshape(1, D_MODEL)
    w_in_b, w_out_b, w1_b, w2_b = (w.astype(BF16) for w in (w_in, w_out, w1, w2))

    x_ctx, x_lat = ctx, x
    for l in range(DEPTH):
        z = _inproj_call(l, x_ctx, x_lat, mods, n1, w_in_b)
        o_f, o_b = _scan_call(z, lb_raw, l)
        x_ctx = x_lat = _tail_call(l, x_ctx, x_lat, o_f, o_b, z, mods, n2, fin, hg, ws, bias,
                                   grp_mean, cw, w_out_b, w1_b, w2_b)
    return x_lat
```

```python
import functools
import math

import jax
import jax.numpy as jnp
from jax import lax
from jax.experimental import pallas as pl
from jax.experimental.pallas import tpu as pltpu

D_MODEL = 1024
DEPTH = 4
CTX_LEN = 256
GRID_W = 64
HG_WIDTH = 512
HG_HEADS = 4
HG_DK = 128
MLP_WIDTH = 256
MLP_GROUPS = 4
MLP_CH = 64
TOK_CHUNK = 128
CONV_WIDTH = 256
D_FF = 4096
N_MOD = 6
EPS = 1e-6
IN_COLS = 3840

TILE = 256
PAIR = 2
CHUNK = 64
N_CHUNK = TILE // CHUNK
MID = CHUNK // 2
SAFE_EXP = 64.0
COND_ROWS = 16
CONV_K_PAD = 8
ADA_TN = 1536
FF_CHUNK = 1024
VMEM_LIMIT = 56 * 1024 * 1024

F32 = jnp.float32
BF16 = jnp.bfloat16


def _dot(a, b):
    return jnp.dot(a, b, preferred_element_type=F32)


def _dot_nt(a, b):
    return lax.dot_general(a, b, (((1,), (1,)), ((), ())), preferred_element_type=F32)


def _dot_tn(a, b):
    return lax.dot_general(a, b, (((0,), (0,)), ((), ())), preferred_element_type=F32)


def _rms(x, gain):
    return x * lax.rsqrt(jnp.mean(x * x, axis=-1, keepdims=True) + EPS) * gain


def _gelu(x):
    c = math.sqrt(2.0 / math.pi)
    return x * (0.5 * (1.0 + jnp.tanh(c * (x + 0.044715 * (x * x * x)))))


def _silu(x):
    return x / (1.0 + jnp.exp(-x))


def _compiler_params():
    return pltpu.CompilerParams(
        dimension_semantics=("arbitrary", "arbitrary"), vmem_limit_bytes=VMEM_LIMIT)


def _ada_kernel(cond_ref, w_ref, b_ref, out_ref):
    s = _silu(cond_ref[...]).astype(BF16)
    out_ref[...] = _dot(s, w_ref[...].astype(BF16)) + b_ref[...]


def _ada_call(cond, ada_w, ada_b):
    n_cols = N_MOD * D_MODEL
    return pl.pallas_call(
        _ada_kernel,
        grid=(DEPTH, n_cols // ADA_TN),
        in_specs=[
            pl.BlockSpec((COND_ROWS, D_MODEL), lambda l, n: (0, 0)),
            pl.BlockSpec((None, D_MODEL, ADA_TN), lambda l, n: (l, 0, n)),
            pl.BlockSpec((None, 1, ADA_TN), lambda l, n: (l, 0, n)),
        ],
        out_specs=pl.BlockSpec((None, COND_ROWS, ADA_TN), lambda l, n: (l, 0, n)),
        out_shape=jax.ShapeDtypeStruct((DEPTH, COND_ROWS, n_cols), F32),
        compiler_params=_compiler_params(),
        name="ada_mod",
    )(cond, ada_w, ada_b.reshape(DEPTH, 1, n_cols)).reshape(DEPTH, COND_ROWS, 1, n_cols)


def _token_specs(layer, n_batch, t0):
    off = 1 if layer == 0 else 0
    ctx_spec = pl.BlockSpec((PAIR, TILE, D_MODEL), lambda b, t: (b, 0, 0))
    lat_spec = pl.BlockSpec((PAIR, TILE, D_MODEL),
                            lambda b, t: (b, jnp.maximum(t + t0 - off, 1 - off), 0))
    mod_spec = pl.BlockSpec(
        (None, PAIR, 1, N_MOD * D_MODEL),
        lambda b, t: (layer, jnp.where(t + t0 == 0, n_batch // PAIR, b), 0, 0))
    return ctx_spec, lat_spec, mod_spec


def _layer_spec(layer, shape):
    return pl.BlockSpec((None,) + shape, lambda b, t: (layer,) + (0,) * len(shape),
                        pipeline_mode=pl.Buffered(1))


def _chunk_cumsum(lf, fwd):
    r = lax.broadcasted_iota(jnp.int32, (CHUNK, 1), 0)
    s = 1
    while s < CHUNK:
        if fwd:
            lf = lf + jnp.where(r >= s, pltpu.roll(lf, s, axis=0), 0.0)
        else:
            lf = lf + jnp.where(r < CHUNK - s, pltpu.roll(lf, CHUNK - s, axis=0), 0.0)
        s *= 2
    return lf


def _inproj_kernel(layer, xc_ref, xl_ref, mod_ref, gain_ref, lbraw_ref, w_ref,
                   zqi_ref, zak_ref, zr_ref):
    is_ctx = pl.program_id(1) == 0
    pair = range(PAIR)
    h = []
    for p in pair:
        x = jnp.where(is_ctx, xc_ref[p], xl_ref[p])
        shift = mod_ref[p, :, 0:D_MODEL]
        scale = mod_ref[p, :, D_MODEL:2 * D_MODEL]
        h.append((_rms(x, gain_ref[...]) * (1.0 + scale) + shift).astype(BF16))

    zak_ref[0, :, 0:2 * HG_WIDTH] = _dot(h[0], w_ref[:, 2 * HG_WIDTH:4 * HG_WIDTH])
    zqi_ref[0] = _dot(h[0], w_ref[:, 0:2 * HG_WIDTH])
    zr_ref[0] = _dot(h[0], w_ref[:, 4 * HG_WIDTH:IN_COLS])
    zak_ref[1, :, 0:2 * HG_WIDTH] = _dot(h[1], w_ref[:, 2 * HG_WIDTH:4 * HG_WIDTH])
    zqi_ref[1] = _dot(h[1], w_ref[:, 0:2 * HG_WIDTH])
    zr_ref[1] = _dot(h[1], w_ref[:, 4 * HG_WIDTH:IN_COLS])

    lbs = []
    for d in range(2):
        rows = [lbraw_ref[d * DEPTH + i:d * DEPTH + i + 1, :] for i in range(DEPTH)]
        top = functools.reduce(jnp.maximum, rows)
        ex = [jnp.exp(row - top) for row in rows]
        part = functools.reduce(lambda u, w: u + w, ex[1:layer + 1], jnp.zeros_like(top))
        lbs.append(part / functools.reduce(lambda u, w: u + w, ex))

    for p in pair:
        for c in range(N_CHUNK):
            rows = slice(c * CHUNK, (c + 1) * CHUNK)
            zs = [[zak_ref[p, rows, d * HG_WIDTH + hh * HG_DK:d * HG_WIDTH + (hh + 1) * HG_DK]
                   for hh in range(HG_HEADS)] for d in range(2)]
            for d in range(2):
                for hh in range(HG_HEADS):
                    ln = slice(hh * HG_DK, (hh + 1) * HG_DK)
                    k = (1.0 - lbs[d][:, ln]) / (1.0 + jnp.exp(zs[d][hh]))
                    a_col = (2 * d) * HG_WIDTH + hh * HG_DK
                    k_col = (2 * d + 1) * HG_WIDTH + hh * HG_DK
                    zak_ref[p, rows, a_col:a_col + HG_DK] = _chunk_cumsum(jnp.log(1.0 - k), d == 0)
                    zak_ref[p, rows, k_col:k_col + HG_DK] = k


def _inproj_call(layer, x_ctx, x_lat, mods, norm1, lb_raw, w_in_bf16):
    n_batch = x_lat.shape[0]
    n_tiles = (CTX_LEN + 4096) // TILE
    ctx_spec, lat_spec, mod_spec = _token_specs(layer, n_batch, 0)
    widths = (2 * HG_WIDTH, 4 * HG_WIDTH, IN_COLS - 4 * HG_WIDTH)
    return pl.pallas_call(
        functools.partial(_inproj_kernel, layer),
        grid=(n_batch // PAIR, n_tiles),
        in_specs=[ctx_spec, lat_spec, mod_spec,
                  _layer_spec(layer, (1, D_MODEL)),
                  pl.BlockSpec((2 * DEPTH, HG_WIDTH), lambda b, t: (0, 0)),
                  _layer_spec(layer, (D_MODEL, IN_COLS))],
        out_specs=[pl.BlockSpec((PAIR, TILE, w), lambda b, t: (b, t, 0)) for w in widths],
        out_shape=[jax.ShapeDtypeStruct((n_batch, n_tiles * TILE, w), F32) for w in widths],
        compiler_params=_compiler_params(),
        name="in_proj",
    )(x_ctx, x_lat, mods, norm1, lb_raw.reshape(2 * DEPTH, HG_WIDTH), w_in_bf16)


def _tri(fwd):
    t = lax.broadcasted_iota(jnp.int32, (CHUNK, CHUNK), 0)
    s = lax.broadcasted_iota(jnp.int32, (CHUNK, CHUNK), 1)
    return (s <= t) if fwd else (s >= t)


def _exact_scores(q, a, ak_ref, row0, h, fwd):
    t_col = lax.broadcasted_iota(jnp.int32, (CHUNK, 1), 0)
    s_row = lax.broadcasted_iota(jnp.int32, (1, CHUNK), 1)
    sub8 = lax.broadcasted_iota(jnp.int32, (8, 1), 0)
    a_ln = slice(h * HG_DK, (h + 1) * HG_DK)
    k_ln = slice(HG_WIDTH + h * HG_DK, HG_WIDTH + (h + 1) * HG_DK)

    def body(s, acc):
        base = pl.multiple_of(row0 + (s // 8) * 8, 8)
        pick = sub8 == (s % 8)
        a_s = jnp.sum(jnp.where(pick, ak_ref[pl.ds(base, 8), a_ln], 0.0), axis=0, keepdims=True)
        k_s = jnp.sum(jnp.where(pick, ak_ref[pl.ds(base, 8), k_ln], 0.0), axis=0, keepdims=True)
        valid = (t_col >= s) if fwd else (t_col <= s)
        w = jnp.exp(jnp.where(valid, a - a_s, 0.0))
        col = jnp.sum(jnp.where(valid, q * w * k_s, 0.0), axis=1, keepdims=True)
        return acc + col * (s_row == s).astype(F32)

    return lax.fori_loop(0, CHUNK, body, jnp.zeros((CHUNK, CHUNK), F32))


def _scan_kernel(qif_ref, akf_ref, qib_ref, akb_ref, of_ref, ob_ref, st_ref):
    @pl.when(pl.program_id(1) == 0)
    def _():
        st_ref[...] = jnp.zeros_like(st_ref)

    worst = jnp.zeros((1, HG_WIDTH), F32)
    for ak_ref in (akf_ref, akb_ref):
        for c in range(N_CHUNK):
            lo, mid, hi = (ak_ref[c * CHUNK + r:c * CHUNK + r + 1, 0:HG_WIDTH]
                           for r in (0, MID, CHUNK - 1))
            worst = jnp.maximum(worst, jnp.maximum(jnp.abs(lo - mid), jnp.abs(hi - mid)))
    safe = jnp.max(worst) <= SAFE_EXP

    units = [(d, h) for d in range(2) for h in range(HG_HEADS)]

    def load_step(step, factorised):
        out = []
        for d, h in units:
            fwd = d == 0
            c = step if fwd else N_CHUNK - 1 - step
            rows = slice(c * CHUNK, (c + 1) * CHUNK)
            qv_ref = qif_ref if fwd else qib_ref
            ak_ref = akf_ref if fwd else akb_ref
            q = qv_ref[rows, h * HG_DK:(h + 1) * HG_DK]
            v = qv_ref[rows, HG_WIDTH + h * HG_DK:HG_WIDTH + (h + 1) * HG_DK].astype(BF16)
            a = ak_ref[rows, h * HG_DK:(h + 1) * HG_DK]
            k = ak_ref[rows, HG_WIDTH + h * HG_DK:HG_WIDTH + (h + 1) * HG_DK]
            tot = a[CHUNK - 1:CHUNK, :] if fwd else a[0:1, :]
            if factorised:
                ref_row = a[MID:MID + 1, :]
                e_mid = jnp.exp(a - ref_row)
                q_mid = q * e_mid
                k_mid = k / e_mid
                q_in = q_mid * jnp.exp(ref_row)
                k_end = k_mid * jnp.exp(tot - ref_row)
                p = jnp.where(_tri(fwd), _dot_nt(q_mid.astype(BF16), k_mid.astype(BF16)), 0.0)
            else:
                q_in = q * jnp.exp(a)
                k_end = k * jnp.exp(tot - a)
                p = _exact_scores(q, a, ak_ref, c * CHUNK, h, fwd)
            out.append((q_in.astype(BF16), k_end.astype(BF16), v, jnp.exp(tot), p.astype(BF16)))
        return out

    def chunk_units(factorised):
        states = [st_ref[i] for i in range(len(units))]
        ops = load_step(0, factorised)
        for step in range(N_CHUNK):
            outs = []
            for i in range(len(units)):
                q_in, k_end, v, dec, _ = ops[i]
                outs.append(_dot_nt(q_in, states[i].astype(BF16)))
                states[i] = states[i] * dec + _dot_tn(v, k_end)
            nxt = load_step(step + 1, factorised) if step + 1 < N_CHUNK else None
            for i, (d, h) in enumerate(units):
                c = step if d == 0 else N_CHUNK - 1 - step
                o_ref = of_ref if d == 0 else ob_ref
                o_ref[c * CHUNK:(c + 1) * CHUNK, h * HG_DK:(h + 1) * HG_DK] = (
                    outs[i] + _dot(ops[i][4], ops[i][2]))
            ops = nxt
        for i in range(len(units)):
            st_ref[i] = states[i]

    @pl.when(safe)
    def _():
        chunk_units(True)

    @pl.when(jnp.logical_not(safe))
    def _():
        chunk_units(False)


def _scan_call(z_qi, z_ak):
    n_batch, n_tok, _ = z_qi.shape
    n_tiles = n_tok // TILE

    def bwd_tile(j):
        return jnp.where(j == 0, 0, n_tiles - j)

    wide = 2 * HG_WIDTH
    return pl.pallas_call(
        _scan_kernel,
        grid=(n_batch, n_tiles),
        in_specs=[
            pl.BlockSpec((None, TILE, wide), lambda b, j: (b, j, 0)),
            pl.BlockSpec((None, TILE, wide), lambda b, j: (b, j, 0)),
            pl.BlockSpec((None, TILE, wide), lambda b, j: (b, bwd_tile(j), 0)),
            pl.BlockSpec((None, TILE, wide), lambda b, j: (b, bwd_tile(j), 1)),
        ],
        out_specs=[
            pl.BlockSpec((None, TILE, HG_WIDTH), lambda b, j: (b, j, 0)),
            pl.BlockSpec((None, TILE, HG_WIDTH), lambda b, j: (b, bwd_tile(j), 0)),
        ],
        out_shape=[jax.ShapeDtypeStruct((n_batch, n_tok, HG_WIDTH), F32)] * 2,
        scratch_shapes=[pltpu.VMEM((2 * HG_HEADS, HG_DK, HG_DK), F32)],
        compiler_params=_compiler_params(),
        name="hgrn2_scan",
    )(z_qi, z_ak, z_qi, z_ak)


def _tail_kernel(final, t0, xc_ref, xl_ref, of_ref, ob_ref, zg_ref, zuv_ref, zhb_ref, zc_ref,
                 mod_ref, n2_ref, fin_ref, hg_ref, ws_ref, bs_ref, grp_ref, cw_ref,
                 wout_ref, w1_ref, w2_ref, out_ref):
    is_ctx = (pl.program_id(1) + t0) == 0
    pair = range(PAIR)

    def mod(p, i):
        return mod_ref[p, :, i * D_MODEL:(i + 1) * D_MODEL]

    vf = [_gelu(zuv_ref[p, :, MLP_WIDTH:2 * MLP_WIDTH]) for p in pair]
    ms = []
    for p in pair:
        sq = vf[p] * vf[p]
        sq_hi = sq.astype(BF16)
        sq_lo = (sq - sq_hi.astype(F32)).astype(BF16)
        ms.append(_dot(sq_hi, grp_ref[...]) + _dot(sq_lo, grp_ref[...]))

    og = []
    for p in pair:
        o = of_ref[p] + ob_ref[p]
        heads = []
        for h in range(HG_HEADS):
            oh = o[:, h * HG_DK:(h + 1) * HG_DK]
            heads.append(oh * lax.rsqrt(jnp.mean(oh * oh, axis=-1, keepdims=True) + EPS))
        og.append(jnp.concatenate(heads, axis=-1) * hg_ref[...] * _silu(zg_ref[p]))

    r = lax.broadcasted_iota(jnp.int32, (TILE, 1), 0)
    col = jnp.where(is_ctx, r, r % GRID_W)
    prev_ok = col != 0
    next_ok = col != jnp.where(is_ctx, TILE - 1, GRID_W - 1)
    cv = []
    for p in pair:
        y = zc_ref[p] * zhb_ref[p, :, 0:CONV_WIDTH]
        y_prev = jnp.where(prev_ok, pltpu.roll(y, 1, axis=0), 0.0)
        y_next = jnp.where(next_ok, pltpu.roll(y, TILE - 1, axis=0), 0.0)
        cv.append(zhb_ref[p, :, CONV_WIDTH:2 * CONV_WIDTH] * (
            cw_ref[0:1, :] * y_prev + cw_ref[1:2, :] * y + cw_ref[2:3, :] * y_next))

    ws = ws_ref[...].astype(BF16)
    n_tok_chunk = TILE // TOK_CHUNK
    full = []
    for p in pair:
        vn = (vf[p] * lax.rsqrt(ms[p] + EPS)).astype(BF16)
        full.append([_dot(ws, vn[cc * TOK_CHUNK:(cc + 1) * TOK_CHUNK, :])
                     for cc in range(n_tok_chunk)])
    lane_grp = lax.broadcasted_iota(jnp.int32, (TOK_CHUNK, MLP_WIDTH), 1) // MLP_CH
    mix = []
    for p in pair:
        uf = _gelu(zuv_ref[p, :, 0:MLP_WIDTH])
        mixed = []
        for cc in range(n_tok_chunk):
            s = bs_ref[...]
            for g in range(MLP_GROUPS):
                s = s + jnp.where(lane_grp == g,
                                  full[p][cc][g * TOK_CHUNK:(g + 1) * TOK_CHUNK, :], 0.0)
            mixed.append(uf[cc * TOK_CHUNK:(cc + 1) * TOK_CHUNK, :] * s)
        m = jnp.concatenate(mixed, axis=0)
        mix.append(jnp.concatenate([og[p], m, cv[p]], axis=-1).astype(BF16))

    proj = [_dot(mix[p], wout_ref[...]) for p in pair]
    x1, h2 = [], []
    for p in pair:
        x = jnp.where(is_ctx, xc_ref[p], xl_ref[p])
        x1.append(x + mod(p, 2) * proj[p])
        h2.append((_rms(x1[p], n2_ref[...]) * (1.0 + mod(p, 4)) + mod(p, 3)).astype(BF16))

    jobs = [(p, f) for f in range(D_FF // FF_CHUNK) for p in pair]
    acc = [jnp.zeros((TILE, D_MODEL), F32) for _ in pair]
    pending = None
    for job in jobs + [None]:
        issued = None
        if job is not None:
            p, f = job
            a = jnp.maximum(_dot(h2[p], w1_ref[:, f * FF_CHUNK:(f + 1) * FF_CHUNK]), 0.0)
            issued = (p, f, (a * a).astype(BF16))
        if pending is not None:
            p, f, act = pending
            acc[p] = acc[p] + _dot(act, w2_ref[f * FF_CHUNK:(f + 1) * FF_CHUNK, :])
        pending = issued

    for p in pair:
        x2 = x1[p] + mod(p, 5) * acc[p]
        if final:
            x2 = _rms(x2, fin_ref[...])
        out_ref[p] = x2


def _tail_call(layer, x_ctx, x_lat, o_f, o_b, z, mods, norm2, fin, hg_gain, tok_ws, bias, grp,
               conv_w, wout, w1, w2):
    n_batch = x_lat.shape[0]
    final = layer == DEPTH - 1
    t0 = 1 if final else 0
    n_tiles = (CTX_LEN + 4096) // TILE - t0
    ctx_spec, lat_spec, mod_spec = _token_specs(layer, n_batch, t0)

    def tile(width, col_blk):
        return pl.BlockSpec((PAIR, TILE, width), lambda b, t: (b, t + t0, col_blk))

    return pl.pallas_call(
        functools.partial(_tail_kernel, final, t0),
        grid=(n_batch // PAIR, n_tiles),
        in_specs=[
            ctx_spec, lat_spec,
            tile(HG_WIDTH, 0),
            tile(HG_WIDTH, 0),
            tile(HG_WIDTH, 0),
            tile(2 * MLP_WIDTH, 1),
            tile(2 * CONV_WIDTH, 2),
            tile(CONV_WIDTH, 6),
            mod_spec,
            _layer_spec(layer, (1, D_MODEL)),
            pl.BlockSpec((1, D_MODEL), lambda b, t: (0, 0)),
            _layer_spec(layer, (1, HG_WIDTH)),
            _layer_spec(layer, (MLP_GROUPS * TOK_CHUNK, TOK_CHUNK)),
            _layer_spec(layer, (TOK_CHUNK, MLP_WIDTH)),
            pl.BlockSpec((MLP_WIDTH, MLP_WIDTH), lambda b, t: (0, 0)),
            _layer_spec(layer, (CONV_K_PAD, CONV_WIDTH)),
            _layer_spec(layer, (D_MODEL, D_MODEL)),
            _layer_spec(layer, (D_MODEL, D_FF)),
            _layer_spec(layer, (D_FF, D_MODEL)),
        ],
        out_specs=pl.BlockSpec((PAIR, TILE, D_MODEL), lambda b, t: (b, t, 0)),
        out_shape=jax.ShapeDtypeStruct((n_batch, n_tiles * TILE, D_MODEL), F32),
        compiler_params=_compiler_params(),
        name="mixer_tail",
    )(x_ctx, x_lat, o_f, o_b, z, z, z, z, mods, norm2, fin, hg_gain, tok_ws, bias, grp, conv_w,
      wout, w1, w2)


def kernel(x, c, ctx, c_ctx, norm1, norm2, ada_w, ada_b, w_in, lb_raw, hg_gain, tok_ws, tok_bs,
           conv_w, w_out, w1, w2, final_norm):
    n_batch = x.shape[0]
    assert x.shape[1:] == (4096, D_MODEL) and ctx.shape[1:] == (CTX_LEN, D_MODEL)
    assert n_batch % PAIR == 0 and n_batch + PAIR <= COND_ROWS and CTX_LEN == TILE

    cond = jnp.zeros((COND_ROWS, D_MODEL), F32).at[:n_batch].set(c)
    cond = cond.at[n_batch:n_batch + PAIR].set(jnp.broadcast_to(c_ctx, (PAIR, D_MODEL)))
    mods = _ada_call(cond, ada_w, ada_b)

    grp = jnp.arange(MLP_WIDTH) // MLP_CH
    grp_mean = jnp.where(grp[:, None] == grp[None, :], 1.0 / MLP_CH, 0.0).astype(BF16)
    bias = jnp.repeat(jnp.swapaxes(tok_bs, 1, 2), MLP_CH, axis=2)
    ws = tok_ws.reshape(DEPTH, MLP_GROUPS * TOK_CHUNK, TOK_CHUNK)
    cw = jnp.zeros((DEPTH, CONV_K_PAD, CONV_WIDTH), F32).at[:, :3].set(conv_w)
    n1 = norm1.reshape(DEPTH, 1, D_MODEL)
    n2 = norm2.reshape(DEPTH, 1, D_MODEL)
    hg = hg_gain.reshape(DEPTH, 1, HG_WIDTH)
    fin = final_norm.reshape(1, D_MODEL)
    w_in_b, w_out_b, w1_b, w2_b = (w.astype(BF16) for w in (w_in, w_out, w1, w2))

    x_ctx, x_lat = ctx, x
    for l in range(DEPTH):
        z_qi, z_ak, z = _inproj_call(l, x_ctx, x_lat, mods, n1, lb_raw, w_in_b)
        o_f, o_b = _scan_call(z_qi, z_ak)
        x_ctx = x_lat = _tail_call(l, x_ctx, x_lat, o_f, o_b, z, mods, n2, fin, hg, ws, bias,
                                   grp_mean, cw, w_out_b, w1_b, w2_b)
    return x_lat
```

```python
import functools
import math

import jax
import jax.numpy as jnp
from jax import lax
from jax.experimental import pallas as pl
from jax.experimental.pallas import tpu as pltpu

D_MODEL = 1024
DEPTH = 4
CTX_LEN = 256
GRID_W = 64
HG_WIDTH = 512
HG_HEADS = 4
HG_DK = 128
MLP_WIDTH = 256
MLP_GROUPS = 4
MLP_CH = 64
TOK_CHUNK = 128
CONV_WIDTH = 256
D_FF = 4096
N_MOD = 6
EPS = 1e-6
IN_COLS = 3840
Z_COLS = IN_COLS - HG_WIDTH

TILE = 256
PAIR = 2
CHUNK = 64
N_CHUNK = TILE // CHUNK
MID = CHUNK // 2
SAFE_EXP = 64.0
COND_ROWS = 16
CONV_K_PAD = 8
ADA_TN = 1536
FF_CHUNK = 1024
VMEM_LIMIT = 56 * 1024 * 1024

F32 = jnp.float32
BF16 = jnp.bfloat16


def _dot(a, b):
    return jnp.dot(a, b, preferred_element_type=F32)


def _dot_nt(a, b):
    return lax.dot_general(a, b, (((1,), (1,)), ((), ())), preferred_element_type=F32)


def _dot_tn(a, b):
    return lax.dot_general(a, b, (((0,), (0,)), ((), ())), preferred_element_type=F32)


def _rms(x, gain):
    return x * lax.rsqrt(jnp.mean(x * x, axis=-1, keepdims=True) + EPS) * gain


def _gelu(x):
    c = math.sqrt(2.0 / math.pi)
    return x * (0.5 * (1.0 + jnp.tanh(c * (x + 0.044715 * (x * x * x)))))


def _silu(x):
    return x / (1.0 + jnp.exp(-x))


def _compiler_params():
    return pltpu.CompilerParams(
        dimension_semantics=("arbitrary", "arbitrary"), vmem_limit_bytes=VMEM_LIMIT)


def _ada_kernel(cond_ref, w_ref, b_ref, out_ref):
    s = _silu(cond_ref[...]).astype(BF16)
    out_ref[...] = _dot(s, w_ref[...].astype(BF16)) + b_ref[...]


def _ada_call(cond, ada_w, ada_b):
    n_cols = N_MOD * D_MODEL
    return pl.pallas_call(
        _ada_kernel,
        grid=(DEPTH, n_cols // ADA_TN),
        in_specs=[
            pl.BlockSpec((COND_ROWS, D_MODEL), lambda l, n: (0, 0)),
            pl.BlockSpec((None, D_MODEL, ADA_TN), lambda l, n: (l, 0, n)),
            pl.BlockSpec((None, 1, ADA_TN), lambda l, n: (l, 0, n)),
        ],
        out_specs=pl.BlockSpec((None, COND_ROWS, ADA_TN), lambda l, n: (l, 0, n)),
        out_shape=jax.ShapeDtypeStruct((DEPTH, COND_ROWS, n_cols), F32),
        compiler_params=_compiler_params(),
        name="ada_mod",
    )(cond, ada_w, ada_b.reshape(DEPTH, 1, n_cols)).reshape(DEPTH, COND_ROWS, 1, n_cols)


def _token_specs(layer, n_batch, t0):
    off = 1 if layer == 0 else 0
    ctx_spec = pl.BlockSpec((PAIR, TILE, D_MODEL), lambda b, t: (b, 0, 0))
    lat_spec = pl.BlockSpec((PAIR, TILE, D_MODEL),
                            lambda b, t: (b, jnp.maximum(t + t0 - off, 1 - off), 0))
    mod_spec = pl.BlockSpec(
        (None, PAIR, 1, N_MOD * D_MODEL),
        lambda b, t: (layer, jnp.where(t + t0 == 0, n_batch // PAIR, b), 0, 0))
    return ctx_spec, lat_spec, mod_spec


def _layer_spec(layer, shape):
    return pl.BlockSpec((None,) + shape, lambda b, t: (layer,) + (0,) * len(shape),
                        pipeline_mode=pl.Buffered(1))


def _inproj_kernel(xc_ref, xl_ref, mod_ref, gain_ref, w_ref, z_ref, zi_ref):
    is_ctx = pl.program_id(1) == 0
    for p in range(PAIR):
        x = jnp.where(is_ctx, xc_ref[p], xl_ref[p])
        shift = mod_ref[p, :, 0:D_MODEL]
        scale = mod_ref[p, :, D_MODEL:2 * D_MODEL]
        h = (_rms(x, gain_ref[...]) * (1.0 + scale) + shift).astype(BF16)
        z_ref[p] = _dot(h, w_ref[:, 0:Z_COLS])
        zi_ref[p] = _dot(h, w_ref[:, Z_COLS:IN_COLS]).astype(BF16)


def _inproj_call(layer, x_ctx, x_lat, mods, norm1, w_in_bf16):
    n_batch = x_lat.shape[0]
    n_tiles = (CTX_LEN + 4096) // TILE
    ctx_spec, lat_spec, mod_spec = _token_specs(layer, n_batch, 0)
    return pl.pallas_call(
        _inproj_kernel,
        grid=(n_batch // PAIR, n_tiles),
        in_specs=[ctx_spec, lat_spec, mod_spec,
                  _layer_spec(layer, (1, D_MODEL)),
                  _layer_spec(layer, (D_MODEL, IN_COLS))],
        out_specs=[pl.BlockSpec((PAIR, TILE, Z_COLS), lambda b, t: (b, t, 0)),
                   pl.BlockSpec((PAIR, TILE, HG_WIDTH), lambda b, t: (b, t, 0))],
        out_shape=[jax.ShapeDtypeStruct((n_batch, n_tiles * TILE, Z_COLS), F32),
                   jax.ShapeDtypeStruct((n_batch, n_tiles * TILE, HG_WIDTH), BF16)],
        compiler_params=_compiler_params(),
        name="in_proj",
    )(x_ctx, x_lat, mods, norm1, w_in_bf16)


def _tri(fwd):
    t = lax.broadcasted_iota(jnp.int32, (CHUNK, CHUNK), 0)
    s = lax.broadcasted_iota(jnp.int32, (CHUNK, CHUNK), 1)
    return (s <= t) if fwd else (s >= t)


def _decay_prep(z, lb, fwd):
    k = (1.0 - lb) / (1.0 + jnp.exp(z))
    lf = jnp.log(1.0 - k)
    hi = lf.astype(BF16)
    r1 = lf - hi.astype(F32)
    mid = r1.astype(BF16)
    lo = (r1 - mid.astype(F32)).astype(BF16)
    tri = _tri(fwd).astype(BF16)
    return _dot(tri, hi) + _dot(tri, mid) + _dot(tri, lo), k


def _exact_scores(q, a, a_ref, k_ref, d, row0, ln, fwd):
    t_col = lax.broadcasted_iota(jnp.int32, (CHUNK, 1), 0)
    s_row = lax.broadcasted_iota(jnp.int32, (1, CHUNK), 1)
    sub8 = lax.broadcasted_iota(jnp.int32, (8, 1), 0)

    def body(s, acc):
        base = pl.multiple_of(row0 + (s // 8) * 8, 8)
        pick = sub8 == (s % 8)
        a_s = jnp.sum(jnp.where(pick, a_ref[d, pl.ds(base, 8), ln], 0.0), axis=0, keepdims=True)
        k_s = jnp.sum(jnp.where(pick, k_ref[d, pl.ds(base, 8), ln], 0.0), axis=0, keepdims=True)
        valid = (t_col >= s) if fwd else (t_col <= s)
        w = jnp.exp(jnp.where(valid, a - a_s, 0.0))
        col = jnp.sum(jnp.where(valid, q * w * k_s, 0.0), axis=1, keepdims=True)
        return acc + col * (s_row == s).astype(F32)

    return lax.fori_loop(0, CHUNK, body, jnp.zeros((CHUNK, CHUNK), F32))


def _scan_kernel(layer, qf_ref, if_ref, qb_ref, ib_ref, zff_ref, zfb_ref, lbraw_ref,
                 of_ref, ob_ref, st_ref, lb_ref, a_ref, k_ref):
    @pl.when(pl.program_id(1) == 0)
    def _():
        st_ref[...] = jnp.zeros_like(st_ref)
        for d in range(2):
            rows = [lbraw_ref[d * DEPTH + i:d * DEPTH + i + 1, :] for i in range(DEPTH)]
            top = functools.reduce(jnp.maximum, rows)
            ex = [jnp.exp(row - top) for row in rows]
            part = functools.reduce(lambda u, w: u + w, ex[1:layer + 1], jnp.zeros_like(top))
            lb_ref[d:d + 1, :] = part / functools.reduce(lambda u, w: u + w, ex)

    worst = jnp.zeros((1, HG_WIDTH), F32)
    for c in range(N_CHUNK):
        rows = slice(c * CHUNK, (c + 1) * CHUNK)
        for d, z_ref in enumerate((zff_ref, zfb_ref)):
            a, k = _decay_prep(z_ref[rows, :], lb_ref[d:d + 1, :], d == 0)
            a_ref[d, rows, :] = a
            k_ref[d, rows, :] = k
            mid = a[MID:MID + 1, :]
            worst = jnp.maximum(worst, jnp.maximum(jnp.abs(a[0:1, :] - mid),
                                                   jnp.abs(a[CHUNK - 1:CHUNK, :] - mid)))
    safe = jnp.max(worst) <= SAFE_EXP
    units = [(d, h) for d in range(2) for h in range(HG_HEADS)]

    def load_step(step, factorised):
        out = []
        for d, h in units:
            fwd = d == 0
            c = step if fwd else N_CHUNK - 1 - step
            rows = slice(c * CHUNK, (c + 1) * CHUNK)
            ln = slice(h * HG_DK, (h + 1) * HG_DK)
            q = (qf_ref if fwd else qb_ref)[rows, ln]
            v = (if_ref if fwd else ib_ref)[rows, ln]
            a = a_ref[d, rows, ln]
            k = k_ref[d, rows, ln]
            tot = a[CHUNK - 1:CHUNK, :] if fwd else a[0:1, :]
            if factorised:
                ref_row = a[MID:MID + 1, :]
                e_mid = jnp.exp(a - ref_row)
                q_mid = q * e_mid
                k_mid = k / e_mid
                q_in = q_mid * jnp.exp(ref_row)
                k_end = k_mid * jnp.exp(tot - ref_row)
                p = jnp.where(_tri(fwd), _dot_nt(q_mid.astype(BF16), k_mid.astype(BF16)), 0.0)
            else:
                q_in = q * jnp.exp(a)
                k_end = k * jnp.exp(tot - a)
                p = _exact_scores(q, a, a_ref, k_ref, d, c * CHUNK, ln, fwd)
            out.append((q_in.astype(BF16), k_end.astype(BF16), v, jnp.exp(tot), p.astype(BF16)))
        return out

    def chunk_units(factorised):
        states = [st_ref[i] for i in range(len(units))]
        ops = load_step(0, factorised)
        for step in range(N_CHUNK):
            outs = []
            for i in range(len(units)):
                q_in, k_end, v, dec, _ = ops[i]
                outs.append(_dot_nt(q_in, states[i].astype(BF16)))
                states[i] = states[i] * dec + _dot_tn(v, k_end)
            ahead = load_step(step + 1, factorised) if step + 1 < N_CHUNK else None
            for i, (d, h) in enumerate(units):
                c = step if d == 0 else N_CHUNK - 1 - step
                o_ref = of_ref if d == 0 else ob_ref
                o_ref[c * CHUNK:(c + 1) * CHUNK, h * HG_DK:(h + 1) * HG_DK] = (
                    outs[i] + _dot(ops[i][4], ops[i][2]))
            ops = ahead
        for i in range(len(units)):
            st_ref[i] = states[i]

    @pl.when(safe)
    def _():
        chunk_units(True)

    @pl.when(jnp.logical_not(safe))
    def _():
        chunk_units(False)


def _scan_call(z, z_i, lb_raw, layer):
    n_batch, n_tok, _ = z.shape
    n_tiles = n_tok // TILE

    def fwd_tile(j):
        return j

    def bwd_tile(j):
        return jnp.where(j == 0, 0, n_tiles - j)

    def spec(tile_fn, col_blk):
        return pl.BlockSpec((None, TILE, HG_WIDTH), lambda b, j: (b, tile_fn(j), col_blk))

    return pl.pallas_call(
        functools.partial(_scan_kernel, layer),
        grid=(n_batch, n_tiles),
        in_specs=[
            spec(fwd_tile, 0),
            spec(fwd_tile, 0),
            spec(bwd_tile, 0),
            spec(bwd_tile, 0),
            spec(fwd_tile, 1),
            spec(bwd_tile, 2),
            pl.BlockSpec((2 * DEPTH, HG_WIDTH), lambda b, j: (0, 0)),
        ],
        out_specs=[spec(fwd_tile, 0), spec(bwd_tile, 0)],
        out_shape=[jax.ShapeDtypeStruct((n_batch, n_tok, HG_WIDTH), F32)] * 2,
        scratch_shapes=[
            pltpu.VMEM((2 * HG_HEADS, HG_DK, HG_DK), F32),
            pltpu.VMEM((2, HG_WIDTH), F32),
            pltpu.VMEM((2, TILE, HG_WIDTH), F32),
            pltpu.VMEM((2, TILE, HG_WIDTH), F32),
        ],
        compiler_params=_compiler_params(),
        name="hgrn2_scan",
    )(z, z_i, z, z_i, z, z, lb_raw.reshape(2 * DEPTH, HG_WIDTH))


def _tail_kernel(final, t0, xc_ref, xl_ref, of_ref, ob_ref, zg_ref, zuv_ref, zhb_ref, zc_ref,
                 mod_ref, n2_ref, fin_ref, hg_ref, ws_ref, bs_ref, grp_ref, cw_ref,
                 wout_ref, w1_ref, w2_ref, out_ref):
    is_ctx = (pl.program_id(1) + t0) == 0
    pair = range(PAIR)

    def mod(p, i):
        return mod_ref[p, :, i * D_MODEL:(i + 1) * D_MODEL]

    vf = [_gelu(zuv_ref[p, :, MLP_WIDTH:2 * MLP_WIDTH]) for p in pair]
    ms = []
    for p in pair:
        sq = vf[p] * vf[p]
        sq_hi = sq.astype(BF16)
        sq_lo = (sq - sq_hi.astype(F32)).astype(BF16)
        ms.append(_dot(sq_hi, grp_ref[...]) + _dot(sq_lo, grp_ref[...]))

    og = []
    for p in pair:
        o = of_ref[p] + ob_ref[p]
        heads = []
        for h in range(HG_HEADS):
            oh = o[:, h * HG_DK:(h + 1) * HG_DK]
            heads.append(oh * lax.rsqrt(jnp.mean(oh * oh, axis=-1, keepdims=True) + EPS))
        og.append(jnp.concatenate(heads, axis=-1) * hg_ref[...] * _silu(zg_ref[p]))

    r = lax.broadcasted_iota(jnp.int32, (TILE, 1), 0)
    col = jnp.where(is_ctx, r, r % GRID_W)
    prev_ok = col != 0
    next_ok = col != jnp.where(is_ctx, TILE - 1, GRID_W - 1)
    cv = []
    for p in pair:
        y = zc_ref[p] * zhb_ref[p, :, 0:CONV_WIDTH]
        y_prev = jnp.where(prev_ok, pltpu.roll(y, 1, axis=0), 0.0)
        y_next = jnp.where(next_ok, pltpu.roll(y, TILE - 1, axis=0), 0.0)
        cv.append(zhb_ref[p, :, CONV_WIDTH:2 * CONV_WIDTH] * (
            cw_ref[0:1, :] * y_prev + cw_ref[1:2, :] * y + cw_ref[2:3, :] * y_next))

    ws = ws_ref[...].astype(BF16)
    n_tok_chunk = TILE // TOK_CHUNK
    full = []
    for p in pair:
        vn = (vf[p] * lax.rsqrt(ms[p] + EPS)).astype(BF16)
        full.append([_dot(ws, vn[cc * TOK_CHUNK:(cc + 1) * TOK_CHUNK, :])
                     for cc in range(n_tok_chunk)])
    lane_grp = lax.broadcasted_iota(jnp.int32, (TOK_CHUNK, MLP_WIDTH), 1) // MLP_CH
    mix = []
    for p in pair:
        uf = _gelu(zuv_ref[p, :, 0:MLP_WIDTH])
        mixed = []
        for cc in range(n_tok_chunk):
            s = bs_ref[...]
            for g in range(MLP_GROUPS):
                s = s + jnp.where(lane_grp == g,
                                  full[p][cc][g * TOK_CHUNK:(g + 1) * TOK_CHUNK, :], 0.0)
            mixed.append(uf[cc * TOK_CHUNK:(cc + 1) * TOK_CHUNK, :] * s)
        m = jnp.concatenate(mixed, axis=0)
        mix.append(jnp.concatenate([og[p], m, cv[p]], axis=-1).astype(BF16))

    proj = [_dot(mix[p], wout_ref[...]) for p in pair]
    x1, h2 = [], []
    for p in pair:
        x = jnp.where(is_ctx, xc_ref[p], xl_ref[p])
        x1.append(x + mod(p, 2) * proj[p])
        h2.append((_rms(x1[p], n2_ref[...]) * (1.0 + mod(p, 4)) + mod(p, 3)).astype(BF16))

    jobs = [(p, f) for f in range(D_FF // FF_CHUNK) for p in pair]
    acc = [jnp.zeros((TILE, D_MODEL), F32) for _ in pair]
    pending = None
    for job in jobs + [None]:
        issued = None
        if job is not None:
            p, f = job
            a = jnp.maximum(_dot(h2[p], w1_ref[:, f * FF_CHUNK:(f + 1) * FF_CHUNK]), 0.0)
            issued = (p, f, (a * a).astype(BF16))
        if pending is not None:
            p, f, act = pending
            acc[p] = acc[p] + _dot(act, w2_ref[f * FF_CHUNK:(f + 1) * FF_CHUNK, :])
        pending = issued

    for p in pair:
        x2 = x1[p] + mod(p, 5) * acc[p]
        if final:
            x2 = _rms(x2, fin_ref[...])
        out_ref[p] = x2


def _tail_call(layer, x_ctx, x_lat, o_f, o_b, z, mods, norm2, fin, hg_gain, tok_ws, bias, grp,
               conv_w, wout, w1, w2):
    n_batch = x_lat.shape[0]
    final = layer == DEPTH - 1
    t0 = 1 if final else 0
    n_tiles = (CTX_LEN + 4096) // TILE - t0
    ctx_spec, lat_spec, mod_spec = _token_specs(layer, n_batch, t0)

    def tile(width, col_blk):
        return pl.BlockSpec((PAIR, TILE, width), lambda b, t: (b, t + t0, col_blk))

    return pl.pallas_call(
        functools.partial(_tail_kernel, final, t0),
        grid=(n_batch // PAIR, n_tiles),
        in_specs=[
            ctx_spec, lat_spec,
            tile(HG_WIDTH, 0),
            tile(HG_WIDTH, 0),
            tile(HG_WIDTH, 3),
            tile(2 * MLP_WIDTH, 4),
            tile(2 * CONV_WIDTH, 5),
            tile(CONV_WIDTH, 12),
            mod_spec,
            _layer_spec(layer, (1, D_MODEL)),
            pl.BlockSpec((1, D_MODEL), lambda b, t: (0, 0)),
            _layer_spec(layer, (1, HG_WIDTH)),
            _layer_spec(layer, (MLP_GROUPS * TOK_CHUNK, TOK_CHUNK)),
            _layer_spec(layer, (TOK_CHUNK, MLP_WIDTH)),
            pl.BlockSpec((MLP_WIDTH, MLP_WIDTH), lambda b, t: (0, 0)),
            _layer_spec(layer, (CONV_K_PAD, CONV_WIDTH)),
            _layer_spec(layer, (D_MODEL, D_MODEL)),
            _layer_spec(layer, (D_MODEL, D_FF)),
            _layer_spec(layer, (D_FF, D_MODEL)),
        ],
        out_specs=pl.BlockSpec((PAIR, TILE, D_MODEL), lambda b, t: (b, t, 0)),
        out_shape=jax.ShapeDtypeStruct((n_batch, n_tiles * TILE, D_MODEL), F32),
        compiler_params=_compiler_params(),
        name="mixer_tail",
    )(x_ctx, x_lat, o_f, o_b, z, z, z, z, mods, norm2, fin, hg_gain, tok_ws, bias, grp, conv_w,
      wout, w1, w2)


def kernel(x, c, ctx, c_ctx, norm1, norm2, ada_w, ada_b, w_in, lb_raw, hg_gain, tok_ws, tok_bs,
           conv_w, w_out, w1, w2, final_norm):
    n_batch = x.shape[0]
    assert x.shape[1:] == (4096, D_MODEL) and ctx.shape[1:] == (CTX_LEN, D_MODEL)
    assert n_batch % PAIR == 0 and n_batch + PAIR <= COND_ROWS and CTX_LEN == TILE

    cond = jnp.zeros((COND_ROWS, D_MODEL), F32).at[:n_batch].set(c)
    cond = cond.at[n_batch:n_batch + PAIR].set(jnp.broadcast_to(c_ctx, (PAIR, D_MODEL)))
    mods = _ada_call(cond, ada_w, ada_b)

    grp = jnp.arange(MLP_WIDTH) // MLP_CH
    grp_mean = jnp.where(grp[:, None] == grp[None, :], 1.0 / MLP_CH, 0.0).astype(BF16)
    bias = jnp.repeat(jnp.swapaxes(tok_bs, 1, 2), MLP_CH, axis=2)
    ws = tok_ws.reshape(DEPTH, MLP_GROUPS * TOK_CHUNK, TOK_CHUNK)
    cw = jnp.zeros((DEPTH, CONV_K_PAD, CONV_WIDTH), F32).at[:, :3].set(conv_w)
    n1 = norm1.reshape(DEPTH, 1, D_MODEL)
    n2 = norm2.reshape(DEPTH, 1, D_MODEL)
    hg = hg_gain.reshape(DEPTH, 1, HG_WIDTH)
    fin = final_norm.reshape(1, D_MODEL)
    w_out_b, w1_b, w2_b = (w.astype(BF16) for w in (w_out, w1, w2))
    w_in_b = jnp.concatenate([w_in[:, :, 0:HG_WIDTH], w_in[:, :, 2 * HG_WIDTH:],
                              w_in[:, :, HG_WIDTH:2 * HG_WIDTH]], axis=2).astype(BF16)

    x_ctx, x_lat = ctx, x
    for l in range(DEPTH):
        z, z_i = _inproj_call(l, x_ctx, x_lat, mods, n1, w_in_b)
        o_f, o_b = _scan_call(z, z_i, lb_raw, l)
        x_ctx = x_lat = _tail_call(l, x_ctx, x_lat, o_f, o_b, z, mods, n2, fin, hg, ws, bias,
                                   grp_mean, cw, w_out_b, w1_b, w2_b)
    return x_lat
```

```python
import functools
import math

import jax
import jax.numpy as jnp
from jax import lax
from jax.experimental import pallas as pl
from jax.experimental.pallas import tpu as pltpu

D_MODEL = 1024
DEPTH = 4
CTX_LEN = 256
GRID_W = 64
HG_WIDTH = 512
HG_HEADS = 4
HG_DK = 128
MLP_WIDTH = 256
MLP_GROUPS = 4
MLP_CH = 64
TOK_CHUNK = 128
CONV_WIDTH = 256
D_FF = 4096
N_MOD = 6
EPS = 1e-6
IN_COLS = 3840

TILE = 256
PAIR = 2
CHUNK = 64
N_CHUNK = TILE // CHUNK
MID = CHUNK // 2
SAFE_EXP = 64.0
COND_ROWS = 16
CONV_K_PAD = 8
ADA_TN = 1536
FF_CHUNK = 1024
VMEM_LIMIT = 56 * 1024 * 1024

F32 = jnp.float32
BF16 = jnp.bfloat16


def _dot(a, b):
    return jnp.dot(a, b, preferred_element_type=F32)


def _dot_nt(a, b):
    return lax.dot_general(a, b, (((1,), (1,)), ((), ())), preferred_element_type=F32)


def _dot_tn(a, b):
    return lax.dot_general(a, b, (((0,), (0,)), ((), ())), preferred_element_type=F32)


def _rms(x, gain):
    return x * lax.rsqrt(jnp.mean(x * x, axis=-1, keepdims=True) + EPS) * gain


def _gelu(x):
    c = math.sqrt(2.0 / math.pi)
    return x * (0.5 * (1.0 + jnp.tanh(c * (x + 0.044715 * (x * x * x)))))


def _silu(x):
    return x / (1.0 + jnp.exp(-x))


def _compiler_params():
    return pltpu.CompilerParams(
        dimension_semantics=("arbitrary", "arbitrary"), vmem_limit_bytes=VMEM_LIMIT)


def _ada_kernel(cond_ref, w_ref, b_ref, out_ref):
    s = _silu(cond_ref[...]).astype(BF16)
    out_ref[...] = _dot(s, w_ref[...].astype(BF16)) + b_ref[...]


def _ada_call(cond, ada_w, ada_b):
    n_cols = N_MOD * D_MODEL
    return pl.pallas_call(
        _ada_kernel,
        grid=(DEPTH, n_cols // ADA_TN),
        in_specs=[
            pl.BlockSpec((COND_ROWS, D_MODEL), lambda l, n: (0, 0)),
            pl.BlockSpec((None, D_MODEL, ADA_TN), lambda l, n: (l, 0, n)),
            pl.BlockSpec((None, 1, ADA_TN), lambda l, n: (l, 0, n)),
        ],
        out_specs=pl.BlockSpec((None, COND_ROWS, ADA_TN), lambda l, n: (l, 0, n)),
        out_shape=jax.ShapeDtypeStruct((DEPTH, COND_ROWS, n_cols), F32),
        compiler_params=_compiler_params(),
        name="ada_mod",
    )(cond, ada_w, ada_b.reshape(DEPTH, 1, n_cols)).reshape(DEPTH, COND_ROWS, 1, n_cols)


def _token_specs(layer, n_batch, t0):
    off = 1 if layer == 0 else 0
    ctx_spec = pl.BlockSpec((PAIR, TILE, D_MODEL), lambda b, t: (b, 0, 0))
    lat_spec = pl.BlockSpec((PAIR, TILE, D_MODEL),
                            lambda b, t: (b, jnp.maximum(t + t0 - off, 1 - off), 0))
    mod_spec = pl.BlockSpec(
        (None, PAIR, 1, N_MOD * D_MODEL),
        lambda b, t: (layer, jnp.where(t + t0 == 0, n_batch // PAIR, b), 0, 0))
    return ctx_spec, lat_spec, mod_spec


def _layer_spec(layer, shape):
    return pl.BlockSpec((None,) + shape, lambda b, t: (layer,) + (0,) * len(shape),
                        pipeline_mode=pl.Buffered(1))


def _inproj_kernel(xc_ref, xl_ref, mod_ref, gain_ref, w_ref, z_ref):
    is_ctx = pl.program_id(1) == 0
    for p in range(PAIR):
        x = jnp.where(is_ctx, xc_ref[p], xl_ref[p])
        shift = mod_ref[p, :, 0:D_MODEL]
        scale = mod_ref[p, :, D_MODEL:2 * D_MODEL]
        h = _rms(x, gain_ref[...]) * (1.0 + scale) + shift
        z_ref[p] = _dot(h.astype(BF16), w_ref[...])


def _inproj_call(layer, x_ctx, x_lat, mods, norm1, w_in_bf16):
    n_batch = x_lat.shape[0]
    n_tiles = (CTX_LEN + 4096) // TILE
    ctx_spec, lat_spec, mod_spec = _token_specs(layer, n_batch, 0)
    return pl.pallas_call(
        _inproj_kernel,
        grid=(n_batch // PAIR, n_tiles),
        in_specs=[ctx_spec, lat_spec, mod_spec,
                  _layer_spec(layer, (1, D_MODEL)),
                  _layer_spec(layer, (D_MODEL, IN_COLS))],
        out_specs=pl.BlockSpec((PAIR, TILE, IN_COLS), lambda b, t: (b, t, 0)),
        out_shape=jax.ShapeDtypeStruct((n_batch, n_tiles * TILE, IN_COLS), F32),
        compiler_params=_compiler_params(),
        name="in_proj",
    )(x_ctx, x_lat, mods, norm1, w_in_bf16)


def _tri(fwd):
    t = lax.broadcasted_iota(jnp.int32, (CHUNK, CHUNK), 0)
    s = lax.broadcasted_iota(jnp.int32, (CHUNK, CHUNK), 1)
    return (s <= t) if fwd else (s >= t)


def _decay_prep(z, lb, fwd):
    f = lb + (1.0 - lb) / (1.0 + jnp.exp(-z))
    lf = jnp.log(f)
    k = 1.0 - f
    hi = lf.astype(BF16)
    r1 = lf - hi.astype(F32)
    mid = r1.astype(BF16)
    lo = (r1 - mid.astype(F32)).astype(BF16)
    tri = _tri(fwd).astype(BF16)
    return _dot(tri, hi) + _dot(tri, mid) + _dot(tri, lo), k


def _exact_scores(q, a, a_ref, k_ref, d, row0, ln, fwd):
    t_col = lax.broadcasted_iota(jnp.int32, (CHUNK, 1), 0)
    s_row = lax.broadcasted_iota(jnp.int32, (1, CHUNK), 1)
    sub8 = lax.broadcasted_iota(jnp.int32, (8, 1), 0)

    def body(s, acc):
        base = pl.multiple_of(row0 + (s // 8) * 8, 8)
        pick = sub8 == (s % 8)
        a_s = jnp.sum(jnp.where(pick, a_ref[d, pl.ds(base, 8), ln], 0.0), axis=0, keepdims=True)
        k_s = jnp.sum(jnp.where(pick, k_ref[d, pl.ds(base, 8), ln], 0.0), axis=0, keepdims=True)
        valid = (t_col >= s) if fwd else (t_col <= s)
        w = jnp.exp(jnp.where(valid, a - a_s, 0.0))
        col = jnp.sum(jnp.where(valid, q * w * k_s, 0.0), axis=1, keepdims=True)
        return acc + col * (s_row == s).astype(F32)

    return lax.fori_loop(0, CHUNK, body, jnp.zeros((CHUNK, CHUNK), F32))


def _scan_kernel(layer, zf_ref, zbqi_ref, zbfb_ref, lbraw_ref, of_ref, ob_ref,
                 st_ref, lb_ref, a_ref, k_ref):
    @pl.when(pl.program_id(1) == 0)
    def _():
        st_ref[...] = jnp.zeros_like(st_ref)
        for d in range(2):
            rows = [lbraw_ref[d * DEPTH + i:d * DEPTH + i + 1, :] for i in range(DEPTH)]
            top = functools.reduce(jnp.maximum, rows)
            ex = [jnp.exp(row - top) for row in rows]
            part = functools.reduce(lambda u, w: u + w, ex[1:layer + 1], jnp.zeros_like(top))
            lb_ref[d:d + 1, :] = part / functools.reduce(lambda u, w: u + w, ex)

    worst = jnp.zeros((1, HG_WIDTH), F32)
    for c in range(N_CHUNK):
        rows = slice(c * CHUNK, (c + 1) * CHUNK)
        for d, fwd in enumerate((True, False)):
            z = zf_ref[rows, 2 * HG_WIDTH:3 * HG_WIDTH] if fwd else zbfb_ref[rows, :]
            a, k = _decay_prep(z, lb_ref[d:d + 1, :], fwd)
            a_ref[d, rows, :] = a
            k_ref[d, rows, :] = k
            ref_row = a[MID:MID + 1, :]
            worst = jnp.maximum(worst, jnp.maximum(jnp.abs(a[0:1, :] - ref_row),
                                                   jnp.abs(a[CHUNK - 1:CHUNK, :] - ref_row)))
    safe = jnp.max(worst) <= SAFE_EXP

    units = [(d, h) for d in range(2) for h in range(HG_HEADS)]

    def load_step(step, factorised):
        out = []
        for d, h in units:
            fwd = d == 0
            c = step if fwd else N_CHUNK - 1 - step
            rows = slice(c * CHUNK, (c + 1) * CHUNK)
            ln = slice(h * HG_DK, (h + 1) * HG_DK)
            qv_ref = zf_ref if fwd else zbqi_ref
            q = qv_ref[rows, h * HG_DK:(h + 1) * HG_DK]
            v = qv_ref[rows, HG_WIDTH + h * HG_DK:HG_WIDTH + (h + 1) * HG_DK].astype(BF16)
            a = a_ref[d, rows, ln]
            k = k_ref[d, rows, ln]
            tot = a[CHUNK - 1:CHUNK, :] if fwd else a[0:1, :]
            if factorised:
                ref_row = a[MID:MID + 1, :]
                e_mid = jnp.exp(a - ref_row)
                q_mid = q * e_mid
                k_mid = k / e_mid
                q_in = q_mid * jnp.exp(ref_row)
                k_end = k_mid * jnp.exp(tot - ref_row)
                p = (q_mid.astype(BF16), k_mid.astype(BF16))
            else:
                q_in = q * jnp.exp(a)
                k_end = k * jnp.exp(tot - a)
                p = _exact_scores(q, a, a_ref, k_ref, d, c * CHUNK, ln, fwd)
            out.append((q_in.astype(BF16), k_end.astype(BF16), v, jnp.exp(tot), p))
        return out

    def chunk_units(factorised):
        states = [st_ref[i] for i in range(len(units))]
        ops = load_step(0, factorised)
        for step in range(N_CHUNK):
            scores, outs = [], []
            for i, (d, h) in enumerate(units):
                p = ops[i][4]
                if factorised:
                    p = jnp.where(_tri(d == 0), _dot_nt(p[0], p[1]), 0.0)
                scores.append(p.astype(BF16))
            for i in range(len(units)):
                q_in, k_end, v, dec, _ = ops[i]
                outs.append(_dot_nt(q_in, states[i].astype(BF16)))
                states[i] = states[i] * dec + _dot_tn(v, k_end)
            for i in range(len(units)):
                outs[i] = outs[i] + _dot(scores[i], ops[i][2])
            if step + 1 < N_CHUNK:
                ops = load_step(step + 1, factorised)
            for i, (d, h) in enumerate(units):
                c = step if d == 0 else N_CHUNK - 1 - step
                o_ref = of_ref if d == 0 else ob_ref
                o_ref[c * CHUNK:(c + 1) * CHUNK, h * HG_DK:(h + 1) * HG_DK] = outs[i]
        for i in range(len(units)):
            st_ref[i] = states[i]

    @pl.when(safe)
    def _():
        chunk_units(True)

    @pl.when(jnp.logical_not(safe))
    def _():
        chunk_units(False)


def _scan_call(z, lb_raw, layer):
    n_batch, n_tok, _ = z.shape
    n_tiles = n_tok // TILE

    def bwd_tile(j):
        return jnp.where(j == 0, 0, n_tiles - j)

    return pl.pallas_call(
        functools.partial(_scan_kernel, layer),
        grid=(n_batch, n_tiles),
        in_specs=[
            pl.BlockSpec((None, TILE, 3 * HG_WIDTH), lambda b, j: (b, j, 0)),
            pl.BlockSpec((None, TILE, 2 * HG_WIDTH), lambda b, j: (b, bwd_tile(j), 0)),
            pl.BlockSpec((None, TILE, HG_WIDTH), lambda b, j: (b, bwd_tile(j), 3)),
            pl.BlockSpec((2 * DEPTH, HG_WIDTH), lambda b, j: (0, 0)),
        ],
        out_specs=[
            pl.BlockSpec((None, TILE, HG_WIDTH), lambda b, j: (b, j, 0)),
            pl.BlockSpec((None, TILE, HG_WIDTH), lambda b, j: (b, bwd_tile(j), 0)),
        ],
        out_shape=[jax.ShapeDtypeStruct((n_batch, n_tok, HG_WIDTH), F32)] * 2,
        scratch_shapes=[
            pltpu.VMEM((2 * HG_HEADS, HG_DK, HG_DK), F32),
            pltpu.VMEM((2, HG_WIDTH), F32),
            pltpu.VMEM((2, TILE, HG_WIDTH), F32),
            pltpu.VMEM((2, TILE, HG_WIDTH), F32),
        ],
        compiler_params=_compiler_params(),
        name="hgrn2_scan",
    )(z, z, z, lb_raw.reshape(2 * DEPTH, HG_WIDTH))


def _tail_kernel(final, t0, xc_ref, xl_ref, of_ref, ob_ref, zg_ref, zuv_ref, zhb_ref, zc_ref,
                 mod_ref, n2_ref, fin_ref, hg_ref, ws_ref, bs_ref, grp_ref, cw_ref,
                 wout_ref, w1_ref, w2_ref, out_ref):
    is_ctx = (pl.program_id(1) + t0) == 0
    pair = range(PAIR)

    def mod(p, i):
        return mod_ref[p, :, i * D_MODEL:(i + 1) * D_MODEL]

    vf = [_gelu(zuv_ref[p, :, MLP_WIDTH:2 * MLP_WIDTH]) for p in pair]
    ms = []
    for p in pair:
        sq = vf[p] * vf[p]
        sq_hi = sq.astype(BF16)
        sq_lo = (sq - sq_hi.astype(F32)).astype(BF16)
        ms.append(_dot(sq_hi, grp_ref[...]) + _dot(sq_lo, grp_ref[...]))

    og = []
    for p in pair:
        o = of_ref[p] + ob_ref[p]
        heads = []
        for h in range(HG_HEADS):
            oh = o[:, h * HG_DK:(h + 1) * HG_DK]
            heads.append(oh * lax.rsqrt(jnp.mean(oh * oh, axis=-1, keepdims=True) + EPS))
        og.append(jnp.concatenate(heads, axis=-1) * hg_ref[...] * _silu(zg_ref[p]))

    r = lax.broadcasted_iota(jnp.int32, (TILE, 1), 0)
    col = jnp.where(is_ctx, r, r % GRID_W)
    prev_ok = col != 0
    next_ok = col != jnp.where(is_ctx, TILE - 1, GRID_W - 1)
    cv = []
    for p in pair:
        y = zc_ref[p] * zhb_ref[p, :, 0:CONV_WIDTH]
        y_prev = jnp.where(prev_ok, pltpu.roll(y, 1, axis=0), 0.0)
        y_next = jnp.where(next_ok, pltpu.roll(y, TILE - 1, axis=0), 0.0)
        cv.append(zhb_ref[p, :, CONV_WIDTH:2 * CONV_WIDTH] * (
            cw_ref[0:1, :] * y_prev + cw_ref[1:2, :] * y + cw_ref[2:3, :] * y_next))

    ws = ws_ref[...].astype(BF16)
    n_tok_chunk = TILE // TOK_CHUNK
    full = []
    for p in pair:
        vn = (vf[p] * lax.rsqrt(ms[p] + EPS)).astype(BF16)
        full.append([_dot(ws, vn[cc * TOK_CHUNK:(cc + 1) * TOK_CHUNK, :])
                     for cc in range(n_tok_chunk)])
    lane_grp = lax.broadcasted_iota(jnp.int32, (TOK_CHUNK, MLP_WIDTH), 1) // MLP_CH
    mix = []
    for p in pair:
        uf = _gelu(zuv_ref[p, :, 0:MLP_WIDTH])
        mixed = []
        for cc in range(n_tok_chunk):
            s = bs_ref[...]
            for g in range(MLP_GROUPS):
                s = s + jnp.where(lane_grp == g,
                                  full[p][cc][g * TOK_CHUNK:(g + 1) * TOK_CHUNK, :], 0.0)
            mixed.append(uf[cc * TOK_CHUNK:(cc + 1) * TOK_CHUNK, :] * s)
        m = jnp.concatenate(mixed, axis=0)
        mix.append(jnp.concatenate([og[p], m, cv[p]], axis=-1).astype(BF16))

    proj = [_dot(mix[p], wout_ref[...]) for p in pair]
    x1, h2 = [], []
    for p in pair:
        x = jnp.where(is_ctx, xc_ref[p], xl_ref[p])
        x1.append(x + mod(p, 2) * proj[p])
        h2.append((_rms(x1[p], n2_ref[...]) * (1.0 + mod(p, 4)) + mod(p, 3)).astype(BF16))

    jobs = [(p, f) for f in range(D_FF // FF_CHUNK) for p in pair]
    acc = [jnp.zeros((TILE, D_MODEL), F32) for _ in pair]
    pending = None
    for job in jobs + [None]:
        issued = None
        if job is not None:
            p, f = job
            a = jnp.maximum(_dot(h2[p], w1_ref[:, f * FF_CHUNK:(f + 1) * FF_CHUNK]), 0.0)
            issued = (p, f, (a * a).astype(BF16))
        if pending is not None:
            p, f, act = pending
            acc[p] = acc[p] + _dot(act, w2_ref[f * FF_CHUNK:(f + 1) * FF_CHUNK, :])
        pending = issued

    for p in pair:
        x2 = x1[p] + mod(p, 5) * acc[p]
        if final:
            x2 = _rms(x2, fin_ref[...])
        out_ref[p] = x2


def _tail_call(layer, x_ctx, x_lat, o_f, o_b, z, mods, norm2, fin, hg_gain, tok_ws, bias, grp,
               conv_w, wout, w1, w2):
    n_batch = x_lat.shape[0]
    final = layer == DEPTH - 1
    t0 = 1 if final else 0
    n_tiles = (CTX_LEN + 4096) // TILE - t0
    ctx_spec, lat_spec, mod_spec = _token_specs(layer, n_batch, t0)

    def tile(width, col_blk):
        return pl.BlockSpec((PAIR, TILE, width), lambda b, t: (b, t + t0, col_blk))

    return pl.pallas_call(
        functools.partial(_tail_kernel, final, t0),
        grid=(n_batch // PAIR, n_tiles),
        in_specs=[
            ctx_spec, lat_spec,
            tile(HG_WIDTH, 0),
            tile(HG_WIDTH, 0),
            tile(HG_WIDTH, 4),
            tile(2 * MLP_WIDTH, 5),
            tile(2 * CONV_WIDTH, 6),
            tile(CONV_WIDTH, 14),
            mod_spec,
            _layer_spec(layer, (1, D_MODEL)),
            pl.BlockSpec((1, D_MODEL), lambda b, t: (0, 0)),
            _layer_spec(layer, (1, HG_WIDTH)),
            _layer_spec(layer, (MLP_GROUPS * TOK_CHUNK, TOK_CHUNK)),
            _layer_spec(layer, (TOK_CHUNK, MLP_WIDTH)),
            pl.BlockSpec((MLP_WIDTH, MLP_WIDTH), lambda b, t: (0, 0)),
            _layer_spec(layer, (CONV_K_PAD, CONV_WIDTH)),
            _layer_spec(layer, (D_MODEL, D_MODEL)),
            _layer_spec(layer, (D_MODEL, D_FF)),
            _layer_spec(layer, (D_FF, D_MODEL)),
        ],
        out_specs=pl.BlockSpec((PAIR, TILE, D_MODEL), lambda b, t: (b, t, 0)),
        out_shape=jax.ShapeDtypeStruct((n_batch, n_tiles * TILE, D_MODEL), F32),
        compiler_params=_compiler_params(),
        name="mixer_tail",
    )(x_ctx, x_lat, o_f, o_b, z, z, z, z, mods, norm2, fin, hg_gain, tok_ws, bias, grp, conv_w,
      wout, w1, w2)


def kernel(x, c, ctx, c_ctx, norm1, norm2, ada_w, ada_b, w_in, lb_raw, hg_gain, tok_ws, tok_bs,
           conv_w, w_out, w1, w2, final_norm):
    n_batch = x.shape[0]
    assert x.shape[1:] == (4096, D_MODEL) and ctx.shape[1:] == (CTX_LEN, D_MODEL)
    assert n_batch % PAIR == 0 and n_batch + PAIR <= COND_ROWS and CTX_LEN == TILE

    cond = jnp.zeros((COND_ROWS, D_MODEL), F32).at[:n_batch].set(c)
    cond = cond.at[n_batch:n_batch + PAIR].set(jnp.broadcast_to(c_ctx, (PAIR, D_MODEL)))
    mods = _ada_call(cond, ada_w, ada_b)

    grp = jnp.arange(MLP_WIDTH) // MLP_CH
    grp_mean = jnp.where(grp[:, None] == grp[None, :], 1.0 / MLP_CH, 0.0).astype(BF16)
    bias = jnp.repeat(jnp.swapaxes(tok_bs, 1, 2), MLP_CH, axis=2)
    ws = tok_ws.reshape(DEPTH, MLP_GROUPS * TOK_CHUNK, TOK_CHUNK)
    cw = jnp.zeros((DEPTH, CONV_K_PAD, CONV_WIDTH), F32).at[:, :3].set(conv_w)
    n1 = norm1.reshape(DEPTH, 1, D_MODEL)
    n2 = norm2.reshape(DEPTH, 1, D_MODEL)
    hg = hg_gain.reshape(DEPTH, 1, HG_WIDTH)
    fin = final_norm.reshape(1, D_MODEL)
    w_in_b, w_out_b, w1_b, w2_b = (w.astype(BF16) for w in (w_in, w_out, w1, w2))

    x_ctx, x_lat = ctx, x
    for l in range(DEPTH):
        z = _inproj_call(l, x_ctx, x_lat, mods, n1, w_in_b)
        o_f, o_b = _scan_call(z, lb_raw, l)
        x_ctx = x_lat = _tail_call(l, x_ctx, x_lat, o_f, o_b, z, mods, n2, fin, hg, ws, bias,
                                   grp_mean, cw, w_out_b, w1_b, w2_b)
    return x_lat
```
